```python
import jax
import jax.numpy as jnp
from jax import lax
import numpy as np

D_MODEL = 1024
BATCH = 8
SEQ = 4096
DEPTH = 4

N_HEADS = 16
HEAD_DIM = D_MODEL // N_HEADS
N_KV_GROUPS = 4
HEADS_PER_GROUP = N_HEADS // N_KV_GROUPS
N_BRANCH = 3
CMP_LEN = 32
CMP_STRIDE = 16
CMP_HIDDEN = 4 * HEAD_DIM
SLC_LEN = 64
SLC_TOP_N = 16
WINDOW = 512
Q_BLOCK = 64
D_FF = 2816
CONV_WIDTH = 3
RMS_EPS = 1e-6
NEG_INF = -1e30
FORCE_SCORE = 1e30

kernel_name = "yoco_shortconv_nsa_macaron_trunk"


def rms_norm(x, g):
    x32 = x.astype(jnp.float32)
    y = x32 * lax.rsqrt(jnp.mean(x32 * x32, axis=-1, keepdims=True) + RMS_EPS)
    return (y * g.astype(jnp.float32)).astype(x.dtype)


def masked_softmax(s, mask):
    s = jnp.where(mask, s, NEG_INF)
    m = jnp.max(s, axis=-1, keepdims=True)
    e = jnp.where(mask, jnp.exp(s - m), 0.0)
    return e / jnp.maximum(jnp.sum(e, axis=-1, keepdims=True), 1e-30)


def alibi_slopes():
    h = jnp.arange(1, N_HEADS + 1, dtype=jnp.float32)
    return jnp.exp2(-8.0 * h / N_HEADS).reshape(N_KV_GROUPS, HEADS_PER_GROUP)


def modulate_pre(x, g_pre, shift, scale):
    return rms_norm(x, g_pre) * (1.0 + scale[:, None, :]) + shift[:, None, :]


def gated_post_add(x, y, g_post, gate, weight):
    return x + weight * gate[:, None, :] * rms_norm(y, g_post)


def swiglu_ffn(h, w_in, w_out):
    g, u = jnp.split(h @ w_in, 2, axis=-1)
    return (jax.nn.silu(g) * u) @ w_out


def causal_shift(v, n):
    if n == 0:
        return v
    return jnp.pad(v[:, :-n], ((0, 0), (n, 0), (0, 0)))


def short_conv_mixer(h, w_in, conv_w, w_out):
    b_gate, c_gate, u = jnp.split(h @ w_in, 3, axis=-1)
    v = c_gate * u
    y = sum(conv_w[k] * causal_shift(v, CONV_WIDTH - 1 - k) for k in range(CONV_WIDTH))
    return (b_gate * y) @ w_out


def build_shared_kv(x, c, kv_norm_g, kv_ada_w, kv_ada_b, kv_w, cmp_pos, cmp_w1, cmp_w2):
    B, S, _ = x.shape
    G, dh = N_KV_GROUPS, HEAD_DIM
    shift, scale = jnp.split(jax.nn.silu(c) @ kv_ada_w + kv_ada_b, 2, axis=-1)
    h = modulate_pre(x, kv_norm_g, shift, scale)
    kv = (h @ kv_w).reshape(B, S, N_BRANCH, 2, G, dh)
    kv = jnp.transpose(kv, (2, 3, 0, 4, 1, 5))
    n_cmp = (S - CMP_LEN) // CMP_STRIDE + 1
    tok = jnp.arange(n_cmp)[:, None] * CMP_STRIDE + jnp.arange(CMP_LEN)[None, :]
    blocks = kv[0][:, :, :, tok] + cmp_pos[:, None, None, None]
    blocks = blocks.reshape(2, B, G, n_cmp, CMP_LEN * dh)
    hid = jax.nn.gelu(jnp.einsum('kbgnf,kfh->kbgnh', blocks, cmp_w1))
    kv_cmp = jnp.einsum('kbgnh,khd->kbgnd', hid, cmp_w2)
    kv_slc = kv[1].reshape(2, B, G, S // SLC_LEN, SLC_LEN, dh)
    kv_win = jnp.pad(kv[2], ((0, 0), (0, 0), (0, 0), (WINDOW, 0), (0, 0)))
    return kv_cmp, kv_slc, kv_win


def nsa_mixer(h, kv_cmp, kv_slc, kv_win, w_in, w_out):
    B, S, _ = h.shape
    G, R, dh = N_KV_GROUPS, HEADS_PER_GROUP, HEAD_DIM
    f32 = jnp.float32
    proj = h @ w_in
    q = proj[..., :N_HEADS * dh].reshape(B, S, G, R, dh) * (dh ** -0.5)
    gates = jax.nn.sigmoid(proj[..., N_HEADS * dh:].astype(f32)).reshape(B, S, N_BRANCH, G, R)
    k_cmp, v_cmp = kv_cmp[0], kv_cmp[1]
    k_slc, v_slc = kv_slc[0], kv_slc[1]
    k_win, v_win = kv_win[0], kv_win[1]
    n_cmp = k_cmp.shape[2]
    n_slc = k_slc.shape[2]
    top_n = min(SLC_TOP_N, n_slc)
    cmp_start = jnp.arange(n_cmp) * CMP_STRIDE
    cmp_end = cmp_start + CMP_LEN - 1
    slc_start = jnp.arange(n_slc) * SLC_LEN
    overlap = ((cmp_start[:, None] < slc_start[None, :] + SLC_LEN)
               & (cmp_end[:, None] >= slc_start[None, :])).astype(f32)
    slopes = alibi_slopes()
    sl5 = slopes[None, :, :, None, None]
    bi = jnp.arange(B)[:, None, None, None]
    gi = jnp.arange(G)[None, :, None, None]
    blk = jnp.arange(n_slc)

    def attend_block(qi):
        q0 = qi * Q_BLOCK
        qb = lax.dynamic_slice_in_dim(q, q0, Q_BLOCK, axis=1)
        t = q0 + jnp.arange(Q_BLOCK)
        d_cmp = t[:, None] - cmp_end[None, :]
        s = jnp.einsum('bqgrd,bgnd->bgrqn', qb, k_cmp, preferred_element_type=f32)
        s = s - sl5 * d_cmp.astype(f32)
        p_cmp = masked_softmax(s, d_cmp >= 0)
        o_cmp = jnp.einsum('bgrqn,bgnd->bqgrd', p_cmp.astype(v_cmp.dtype), v_cmp)
        imp = jnp.einsum('bgrqn,nj->bgqj', p_cmp, overlap)
        cur = t // SLC_LEN
        forced = (blk[None, :] == 0) | (blk[None, :] == cur[:, None]) | (blk[None, :] == cur[:, None] - 1)
        future = slc_start[None, :] > t[:, None]
        imp = jnp.where(forced, FORCE_SCORE, jnp.where(future, NEG_INF, imp))
        _, idx = lax.top_k(imp, top_n)
        k_sel = k_slc[bi, gi, idx]
        v_sel = v_slc[bi, gi, idx]
        d_sel = t[None, None, :, None, None] - (idx[..., None] * SLC_LEN + jnp.arange(SLC_LEN))
        s = jnp.einsum('bqgrd,bgqnld->bgrqnl', qb, k_sel, preferred_element_type=f32)
        s = s - slopes[None, :, :, None, None, None] * d_sel[:, :, None].astype(f32)
        s = s.reshape(B, G, R, Q_BLOCK, top_n * SLC_LEN)
        m_sel = (d_sel >= 0)[:, :, None].reshape(B, G, 1, Q_BLOCK, top_n * SLC_LEN)
        p_sel = masked_softmax(s, m_sel).reshape(B, G, R, Q_BLOCK, top_n, SLC_LEN)
        o_sel = jnp.einsum('bgrqnl,bgqnld->bqgrd', p_sel.astype(v_sel.dtype), v_sel)
        kw = lax.dynamic_slice_in_dim(k_win, q0, WINDOW + Q_BLOCK, axis=2)
        vw = lax.dynamic_slice_in_dim(v_win, q0, WINDOW + Q_BLOCK, axis=2)
        spos = q0 - WINDOW + jnp.arange(WINDOW + Q_BLOCK)
        d_win = t[:, None] - spos[None, :]
        m_win = (d_win >= 0) & (d_win < WINDOW) & (spos[None, :] >= 0)
        s = jnp.einsum('bqgrd,bgkd->bgrqk', qb, kw, preferred_element_type=f32)
        s = s - sl5 * d_win.astype(f32)
        p_win = masked_softmax(s, m_win)
        o_win = jnp.einsum('bgrqk,bgkd->bqgrd', p_win.astype(vw.dtype), vw)
        g = lax.dynamic_slice_in_dim(gates, q0, Q_BLOCK, axis=1)
        o = (g[:, :, 0, :, :, None] * o_cmp.astype(f32)
             + g[:, :, 1, :, :, None] * o_sel.astype(f32)
             + g[:, :, 2, :, :, None] * o_win.astype(f32))
        return o.astype(h.dtype)

    o = lax.map(attend_block, jnp.arange(S // Q_BLOCK))
    o = jnp.moveaxis(o, 0, 1).reshape(B, S, N_HEADS * dh)
    return o @ w_out


def setup_inputs(seed: int = 0) -> dict:
    key = jax.random.key(seed)
    ks = jax.random.split(key, 20)
    f32 = jnp.float32
    D = D_MODEL
    n_a = DEPTH // 2
    n_b = DEPTH - n_a
    kv_cols = N_BRANCH * 2 * N_KV_GROUPS * HEAD_DIM

    def dense(k, shape, fan_in):
        return jax.random.normal(k, shape, f32) * fan_in ** -0.5

    def small(k, shape, s):
        return jax.random.normal(k, shape, f32) * s

    return {
        "x": jax.random.normal(ks[0], (BATCH, SEQ, D), f32),
        "c": jax.random.normal(ks[1], (BATCH, D), f32),
        "ada_w": dense(ks[2], (DEPTH, D, 9 * D), D),
        "ada_b": small(ks[3], (DEPTH, 9 * D), 0.01),
        "norm_g": 1.0 + small(ks[4], (DEPTH, 3, 2, D), 0.05),
        "ffn_w_in": dense(ks[5], (DEPTH, 2, D, 2 * D_FF), D),
        "ffn_w_out": dense(ks[6], (DEPTH, 2, D_FF, D), D_FF),
        "a_w_in": dense(ks[7], (n_a, D, 3 * D), D),
        "a_conv": dense(ks[8], (n_a, CONV_WIDTH, D), CONV_WIDTH),
        "a_w_out": dense(ks[9], (n_a, D, D), D),
        "kv_norm_g": 1.0 + small(ks[10], (D,), 0.05),
        "kv_ada_w": dense(ks[11], (D, 2 * D), D),
        "kv_ada_b": small(ks[12], (2 * D,), 0.01),
        "kv_w": dense(ks[13], (D, kv_cols), D),
        "cmp_pos": small(ks[14], (2, CMP_LEN, HEAD_DIM), 0.1),
        "cmp_w1": dense(ks[15], (2, CMP_LEN * HEAD_DIM, CMP_HIDDEN), CMP_LEN * HEAD_DIM),
        "cmp_w2": dense(ks[16], (2, CMP_HIDDEN, HEAD_DIM), CMP_HIDDEN),
        "b_w_in": dense(ks[17], (n_b, D, N_HEADS * HEAD_DIM + N_BRANCH * N_HEADS), D),
        "b_w_out": dense(ks[18], (n_b, N_HEADS * HEAD_DIM, D), N_HEADS * HEAD_DIM),
    }


def reference(x, c, ada_w, ada_b, norm_g, ffn_w_in, ffn_w_out, a_w_in, a_conv, a_w_out,
              kv_norm_g, kv_ada_w, kv_ada_b, kv_w, cmp_pos, cmp_w1, cmp_w2, b_w_in, b_w_out):
    B = x.shape[0]
    D = x.shape[-1]
    n_a = DEPTH // 2
    kv_cmp = kv_slc = kv_win = None
    for layer in range(DEPTH):
        if layer == n_a:
            kv_cmp, kv_slc, kv_win = build_shared_kv(
                x, c, kv_norm_g, kv_ada_w, kv_ada_b, kv_w, cmp_pos, cmp_w1, cmp_w2)
        mod = (jax.nn.silu(c) @ ada_w[layer] + ada_b[layer]).reshape(B, 3, 3, D)
        g = norm_g[layer]
        h = modulate_pre(x, g[0, 0], mod[:, 0, 0], mod[:, 0, 1])
        y = swiglu_ffn(h, ffn_w_in[layer, 0], ffn_w_out[layer, 0])
        x = gated_post_add(x, y, g[0, 1], mod[:, 0, 2], 0.5)
        h = modulate_pre(x, g[1, 0], mod[:, 1, 0], mod[:, 1, 1])
        if layer < n_a:
            y = short_conv_mixer(h, a_w_in[layer], a_conv[layer], a_w_out[layer])
        else:
            y = nsa_mixer(h, kv_cmp, kv_slc, kv_win, b_w_in[layer - n_a], b_w_out[layer - n_a])
        x = gated_post_add(x, y, g[1, 1], mod[:, 1, 2], 1.0)
        h = modulate_pre(x, g[2, 0], mod[:, 2, 0], mod[:, 2, 1])
        y = swiglu_ffn(h, ffn_w_in[layer, 1], ffn_w_out[layer, 1])
        x = gated_post_add(x, y, g[2, 1], mod[:, 2, 2], 0.5)
    return x
```

```python
import functools

import jax
import jax.numpy as jnp
from jax import lax
from jax.experimental import pallas as pl
from jax.experimental.pallas import tpu as pltpu

F32 = jnp.float32
BF16 = jnp.bfloat16

N_HEADS = 16
N_KV_GROUPS = 4
HEADS_PER_GROUP = N_HEADS // N_KV_GROUPS
HEAD_DIM = 64
N_BRANCH = 3
CMP_LEN = 32
CMP_STRIDE = 16
SLC_LEN = 64
SLC_SHIFT = 6
SLC_TOP_N = 16
WINDOW = 512
CONV_WIDTH = 3
RMS_EPS = 1e-6
NEG_INF = -1e30
FORCE_SCORE = 1e30
DEPTH = 4

TQ = 256
KC = 256
SEL_ROWS = 64
ALIBI_ROWS = 16
QSEL_ROWS = 256
QWIN_ROWS = 128
V_ROWS = HEAD_DIM + 16
MASK_BIG = float(2 ** 30)

VMEM_LIMIT = 56 * 1024 * 1024
TOKEN_TILE = 512


def _params(sem):
    return pltpu.CompilerParams(dimension_semantics=sem, vmem_limit_bytes=VMEM_LIMIT)


def _resident(shape):
    nd = len(shape)
    return pl.BlockSpec(shape, lambda *_: (0,) * nd, pipeline_mode=pl.Buffered(1))


def _rms(x, g):
    ms = jnp.mean(x * x, axis=-1, keepdims=True)
    return x * lax.rsqrt(ms + RMS_EPS) * g


def _pre(x, mod_ref, ng_ref, sub):
    shift = mod_ref[0, 3 * sub:3 * sub + 1, :]
    scale = mod_ref[0, 3 * sub + 1:3 * sub + 2, :]
    return _rms(x, ng_ref[2 * sub:2 * sub + 1, :]) * (1.0 + scale) + shift


def _post(x, y, mod_ref, ng_ref, sub, weight):
    gate = mod_ref[0, 3 * sub + 2:3 * sub + 3, :]
    return x + weight * gate * _rms(y, ng_ref[2 * sub + 1:2 * sub + 2, :])


def _dot(a, b):
    return jnp.dot(a, b, preferred_element_type=F32)


def _dot_nt(a, b):
    return lax.dot_general(a, b, (((1,), (1,)), ((), ())), preferred_element_type=F32)


def _dot_tn(a, b):
    return lax.dot_general(a, b, (((0,), (0,)), ((), ())), preferred_element_type=F32)


def _ada_kernel(c_ref, w_ref, b_ref, o_ref):
    c = c_ref[...]
    sc = c * jax.nn.sigmoid(c)
    o_ref[0] = _dot(sc, w_ref[0]) + b_ref[0]


def _ada(c, w, b, tn):
    L, D, N = w.shape
    B = c.shape[0]
    return pl.pallas_call(
        _ada_kernel,
        grid=(L, N // tn),
        in_specs=[
            pl.BlockSpec((B, D), lambda l, j: (0, 0)),
            pl.BlockSpec((1, D, tn), lambda l, j: (l, 0, j)),
            pl.BlockSpec((1, 1, tn), lambda l, j: (l, 0, j)),
        ],
        out_specs=pl.BlockSpec((1, B, tn), lambda l, j: (l, 0, j)),
        out_shape=jax.ShapeDtypeStruct((L, B, N), F32),
        compiler_params=_params(("parallel", "parallel")),
        name="ada",
    )(c, w, b.reshape(L, 1, N))


def _ffn_kernel(x_ref, mod_ref, ng_ref, win_ref, wout_ref, o_ref, *, sub, d_ff, n_chunks):
    x = x_ref[0]
    h = _pre(x, mod_ref, ng_ref, sub).astype(BF16)
    tf = d_ff // n_chunks
    y = None
    for c in range(n_chunks):
        g = _dot(h, win_ref[:, c * tf:(c + 1) * tf])
        u = _dot(h, win_ref[:, d_ff + c * tf:d_ff + (c + 1) * tf])
        a = (g * jax.nn.sigmoid(g) * u).astype(BF16)
        part = _dot(a, wout_ref[c * tf:(c + 1) * tf, :])
        y = part if y is None else y + part
    o_ref[0] = _post(x, y, mod_ref, ng_ref, sub, 0.5)


def _ffn(x, mod, ng, w_in, w_out, sub):
    B, S, D = x.shape
    d_ff = w_out.shape[0]
    tm = min(TOKEN_TILE, S)
    n_chunks = 2 if (d_ff // 2) % 128 == 0 else 1
    return pl.pallas_call(
        functools.partial(_ffn_kernel, sub=sub, d_ff=d_ff, n_chunks=n_chunks),
        grid=(B, S // tm),
        in_specs=[
            pl.BlockSpec((1, tm, D), lambda b, i: (b, i, 0)),
            pl.BlockSpec((1, 9, D), lambda b, i: (b, 0, 0)),
            pl.BlockSpec((6, D), lambda b, i: (0, 0)),
            _resident(w_in.shape),
            _resident(w_out.shape),
        ],
        out_specs=pl.BlockSpec((1, tm, D), lambda b, i: (b, i, 0)),
        out_shape=jax.ShapeDtypeStruct(x.shape, F32),
        compiler_params=_params(("parallel", "parallel")),
        name="ffn",
    )(x, mod, ng, w_in, w_out)


def _conv_kernel(x_ref, mod_ref, ng_ref, win_ref, cw_ref, wout_ref, o_ref, vbuf, *, tm):
    D = x_ref.shape[-1]
    x = x_ref[0]
    h = _pre(x, mod_ref, ng_ref, 1).astype(BF16)
    proj = _dot(h, win_ref[...])
    b_gate = proj[:, :D]
    v = proj[:, D:2 * D] * proj[:, 2 * D:]

    @pl.when(pl.program_id(1) == 0)
    def _():
        vbuf[0:8, :] = jnp.zeros((8, D), F32)

    vbuf[8:8 + tm, :] = v
    y = (cw_ref[2:3, :] * v
         + cw_ref[1:2, :] * vbuf[7:7 + tm, :]
         + cw_ref[0:1, :] * vbuf[6:6 + tm, :])
    vbuf[0:8, :] = v[tm - 8:, :]
    z = (b_gate * y).astype(BF16)
    o_ref[0] = _post(x, _dot(z, wout_ref[...]), mod_ref, ng_ref, 1, 1.0)


def _conv_mixer(x, mod, ng, w_in, conv_w, w_out):
    B, S, D = x.shape
    tm = min(TOKEN_TILE, S)
    return pl.pallas_call(
        functools.partial(_conv_kernel, tm=tm),
        grid=(B, S // tm),
        in_specs=[
            pl.BlockSpec((1, tm, D), lambda b, i: (b, i, 0)),
            pl.BlockSpec((1, 9, D), lambda b, i: (b, 0, 0)),
            pl.BlockSpec((6, D), lambda b, i: (0, 0)),
            _resident(w_in.shape),
            pl.BlockSpec(conv_w.shape, lambda b, i: (0, 0)),
            _resident(w_out.shape),
        ],
        out_specs=pl.BlockSpec((1, tm, D), lambda b, i: (b, i, 0)),
        out_shape=jax.ShapeDtypeStruct(x.shape, F32),
        scratch_shapes=[pltpu.VMEM((tm + 8, D), F32)],
        compiler_params=_params(("parallel", "arbitrary")),
        name="conv_mixer",
    )(x, mod, ng, w_in, conv_w, w_out)


def _pos_cols(shape, first, hi, lo):
    lane = lax.broadcasted_iota(jnp.int32, shape, 1)
    hi_b = jnp.broadcast_to(hi, shape)
    lo_b = jnp.broadcast_to(lo, shape)
    return jnp.where((lane >= first) & (lane < first + 3), hi_b,
                     jnp.where((lane >= first + 3) & (lane < first + 6), lo_b, 0.0))


def _kv_kernel(x_ref, mod_ref, g_ref, w0_ref, wks_ref, wkw_ref, wvsT_ref, wvwT_ref,
               kv0_ref, ksel_ref, kwin_ref, vselT_ref, vwinT_ref, *, tm):
    G = N_KV_GROUPS
    x = x_ref[0]
    h = (_rms(x, g_ref[...]) * (1.0 + mod_ref[0, 1:2, :]) + mod_ref[0, 0:1, :]).astype(BF16)
    kv0_ref[0] = _dot(h, w0_ref[...])

    tok = pl.program_id(1) * tm + lax.broadcasted_iota(jnp.int32, (tm, 1), 0)
    blk = tok >> SLC_SHIFT
    blk64 = (blk * SLC_LEN).astype(F32)
    off = (tok & (SLC_LEN - 1)).astype(F32)
    lane = lax.broadcasted_iota(jnp.int32, (tm, QSEL_ROWS), 1)
    onehot = ((lane - HEAD_DIM) == blk).astype(F32)
    sel_const = onehot + _pos_cols((tm, QSEL_ROWS), HEAD_DIM + SEL_ROWS, blk64, off)
    win_const = _pos_cols((tm, QWIN_ROWS), HEAD_DIM, blk64, off)

    ks = _dot(h, wks_ref[...])
    kw = _dot(h, wkw_ref[...])
    for g in range(G):
        ksel_ref[0, g] = (ks[:, g * QSEL_ROWS:(g + 1) * QSEL_ROWS] + sel_const).astype(BF16)
        kwin_ref[0, g] = (kw[:, g * QWIN_ROWS:(g + 1) * QWIN_ROWS] + win_const).astype(BF16)

    ones_rows = (lax.broadcasted_iota(jnp.int32, (V_ROWS - HEAD_DIM, KC), 0) == 0).astype(F32)
    for wT_ref, out_ref in ((wvsT_ref, vselT_ref), (wvwT_ref, vwinT_ref)):
        vT = _dot_nt(wT_ref[...], h)
        for g in range(G):
            for c in range(tm // KC):
                blk_v = vT[g * HEAD_DIM:(g + 1) * HEAD_DIM, c * KC:(c + 1) * KC]
                out_ref[0, g, c] = jnp.concatenate([blk_v, ones_rows], axis=0).astype(BF16)


def _build_kv(x, kvmod, kv_norm_g, w0, wks, wkw, wvsT, wvwT):
    B, S, D = x.shape
    G = N_KV_GROUPS
    tm = min(TOKEN_TILE, S)
    out_shape = (
        jax.ShapeDtypeStruct((B, S, w0.shape[1]), F32),
        jax.ShapeDtypeStruct((B, G, S, QSEL_ROWS), BF16),
        jax.ShapeDtypeStruct((B, G, S, QWIN_ROWS), BF16),
        jax.ShapeDtypeStruct((B, G, S // KC, V_ROWS, KC), BF16),
        jax.ShapeDtypeStruct((B, G, S // KC, V_ROWS, KC), BF16),
    )
    return pl.pallas_call(
        functools.partial(_kv_kernel, tm=tm),
        grid=(B, S // tm),
        in_specs=[
            pl.BlockSpec((1, tm, D), lambda b, i: (b, i, 0)),
            pl.BlockSpec((1, 2, D), lambda b, i: (b, 0, 0)),
            pl.BlockSpec((1, D), lambda b, i: (0, 0)),
            _resident(w0.shape), _resident(wks.shape), _resident(wkw.shape),
            _resident(wvsT.shape), _resident(wvwT.shape),
        ],
        out_specs=(
            pl.BlockSpec((1, tm, w0.shape[1]), lambda b, i: (b, i, 0)),
            pl.BlockSpec((1, G, tm, QSEL_ROWS), lambda b, i: (b, 0, i, 0)),
            pl.BlockSpec((1, G, tm, QWIN_ROWS), lambda b, i: (b, 0, i, 0)),
            pl.BlockSpec((1, G, tm // KC, V_ROWS, KC), lambda b, i: (b, 0, i, 0, 0)),
            pl.BlockSpec((1, G, tm // KC, V_ROWS, KC), lambda b, i: (b, 0, i, 0, 0)),
        ),
        out_shape=out_shape,
        compiler_params=_params(("parallel", "parallel")),
        name="build_kv",
    )(x, kvmod, kv_norm_g.reshape(1, D), w0, wks, wkw, wvsT, wvwT)


def _gelu_tanh(x):
    return 0.5 * x * (1.0 + jnp.tanh(0.7978845608028654 * (x + 0.044715 * (x * x * x))))


def _cmp_kernel(ck_ref, cv_ref, pos_ref, w1_ref, w2k_ref, w2vT_ref, kc_ref, vcT_ref, *, nc):
    def hidden(c, kv):
        a = _dot((c + pos_ref[2 * kv:2 * kv + 1, :]).astype(BF16), w1_ref[kv, 0])
        b = _dot((c + pos_ref[2 * kv + 1:2 * kv + 2, :]).astype(BF16), w1_ref[kv, 1])
        return _gelu_tanh(a + pltpu.roll(b, nc - 1, 0))

    hk = hidden(ck_ref[0, 0], 0).astype(BF16)
    hv = hidden(cv_ref[0, 0], 1).astype(BF16)
    n = lax.broadcasted_iota(jnp.int32, (nc, 1), 0)
    end_hi = (CMP_STRIDE * (n + 1)).astype(F32)
    end_lo = jnp.full((nc, 1), float(CMP_LEN - 1 - CMP_STRIDE), F32)
    kc = _dot(hk, w2k_ref[...]) + _pos_cols((nc, QWIN_ROWS), HEAD_DIM, end_hi, end_lo)
    kc_ref[0, 0] = kc.astype(BF16)
    vcT_ref[0, 0] = _dot_nt(w2vT_ref[...], hv).astype(BF16)


def _compress(ck, cv, pos, w1, w2k, w2vT):
    B, G, nc, F = ck.shape
    return pl.pallas_call(
        functools.partial(_cmp_kernel, nc=nc),
        grid=(B, G),
        in_specs=[
            pl.BlockSpec((1, 1, nc, F), lambda b, g: (b, g, 0, 0)),
            pl.BlockSpec((1, 1, nc, F), lambda b, g: (b, g, 0, 0)),
            pl.BlockSpec(pos.shape, lambda b, g: (0, 0)),
            _resident(w1.shape), _resident(w2k.shape), _resident(w2vT.shape),
        ],
        out_specs=(
            pl.BlockSpec((1, 1, nc, QWIN_ROWS), lambda b, g: (b, g, 0, 0)),
            pl.BlockSpec((1, 1, HEAD_DIM, nc), lambda b, g: (b, g, 0, 0)),
        ),
        out_shape=(
            jax.ShapeDtypeStruct((B, G, nc, QWIN_ROWS), BF16),
            jax.ShapeDtypeStruct((B, G, HEAD_DIM, nc), BF16),
        ),
        compiler_params=_params(("parallel", "parallel")),
        name="compress_kv",
    )(ck, cv, pos, w1, w2k, w2vT)


def _qproj_kernel(x_ref, mod_ref, ng_ref, wqT_ref, wgT_ref, qT_ref, gT_ref):
    h = _pre(x_ref[0], mod_ref, ng_ref, 1).astype(BF16)
    qT_ref[0] = (_dot_nt(wqT_ref[...], h) * (HEAD_DIM ** -0.5)).astype(BF16)
    gT_ref[0] = jax.nn.sigmoid(_dot_nt(wgT_ref[...], h))


def _qproj(x, mod, ng, wqT, wgT):
    B, S, D = x.shape
    tm = min(TOKEN_TILE, S)
    return pl.pallas_call(
        _qproj_kernel,
        grid=(B, S // tm),
        in_specs=[
            pl.BlockSpec((1, tm, D), lambda b, i: (b, i, 0)),
            pl.BlockSpec((1, 9, D), lambda b, i: (b, 0, 0)),
            pl.BlockSpec((6, D), lambda b, i: (0, 0)),
            _resident(wqT.shape), _resident(wgT.shape),
        ],
        out_specs=(
            pl.BlockSpec((1, wqT.shape[0], tm), lambda b, i: (b, 0, i)),
            pl.BlockSpec((1, wgT.shape[0], tm), lambda b, i: (b, 0, i)),
        ),
        out_shape=(
            jax.ShapeDtypeStruct((B, wqT.shape[0], S), BF16),
            jax.ShapeDtypeStruct((B, wgT.shape[0], S), F32),
        ),
        compiler_params=_params(("parallel", "parallel")),
        name="qproj",
    )(x, mod, ng, wqT, wgT)


def _attn_kernel(qT_ref, gT_ref, kc_ref, vcT_ref, ksel_ref, vselT_ref, kwin_ref, vwinT_ref,
                 arows_ref, ovT_ref, oT_ref, qsel_ref, qwin_ref, m_ref, acc_ref, osel_ref,
                 *, ns, nc, top_n):
    R = HEADS_PER_GROUP
    dh = HEAD_DIM
    qi = pl.program_id(2)
    q0 = qi * TQ
    t_row = q0 + lax.broadcasted_iota(jnp.int32, (1, TQ), 1)

    n_io = lax.broadcasted_iota(jnp.int32, (nc, TQ), 0)
    valid_c = (CMP_STRIDE * n_io + (CMP_LEN - 1)) <= t_row
    kc = kc_ref[0, 0]
    vcT = vcT_ref[0, 0]
    pad_win = jnp.zeros((QWIN_ROWS - dh - ALIBI_ROWS, TQ), BF16)
    pad_sel = jnp.zeros((QSEL_ROWS - dh - SEL_ROWS - ALIBI_ROWS, TQ), BF16)
    psum = jnp.zeros((nc, TQ), F32)
    o_cmp = []
    for r in range(R):
        qTr = qT_ref[0, r * dh:(r + 1) * dh, :]
        qw = jnp.concatenate([qTr, arows_ref[0, r], pad_win], axis=0)
        qwin_ref[:, r * TQ:(r + 1) * TQ] = qw
        s = jnp.where(valid_c, _dot(kc, qw), NEG_INF)
        m = jnp.max(s, axis=0, keepdims=True)
        e = jnp.where(valid_c, jnp.exp(s - m), 0.0)
        p = e / jnp.maximum(jnp.sum(e, axis=0, keepdims=True), 1e-30)
        psum = psum + p
        o_cmp.append(_dot(vcT, p.astype(BF16)))

    p_hi = psum.astype(BF16)
    rem = psum - p_hi.astype(F32)
    p_mid = rem.astype(BF16)
    p_lo = (rem - p_mid.astype(F32)).astype(BF16)
    ovT = ovT_ref[...]
    imp = _dot(ovT, p_hi) + _dot(ovT, p_mid) + _dot(ovT, p_lo)
    j_io = lax.broadcasted_iota(jnp.int32, (ns, TQ), 0)
    cur = t_row >> SLC_SHIFT
    forced = (j_io == 0) | (j_io == cur) | (j_io == cur - 1)
    future = (j_io * SLC_LEN) > t_row
    imp = jnp.where(forced, FORCE_SCORE, jnp.where(future, NEG_INF, imp))

    groups = [imp[8 * k:8 * k + 8, :] for k in range(ns // 8)]
    ranks = [jnp.zeros((8, TQ), jnp.int32) for _ in groups]
    sub_io = lax.broadcasted_iota(jnp.int32, (8, TQ), 0)
    for jp in range(ns):
        row = imp[jp:jp + 1, :]
        for k, grp in enumerate(groups):
            if 8 * k > jp:
                beats = (row >= grp).astype(jnp.int32)
            elif 8 * k + 7 < jp:
                beats = (row > grp).astype(jnp.int32)
            else:
                beats = jnp.where(sub_io > jp - 8 * k, (row >= grp).astype(jnp.int32),
                                  (row > grp).astype(jnp.int32))
            ranks[k] = ranks[k] + beats
    rank = jnp.concatenate(ranks, axis=0)
    sel_rows = jnp.where(rank < top_n, 0.0, -MASK_BIG).astype(BF16)
    if ns < SEL_ROWS:
        sel_rows = jnp.concatenate([sel_rows, jnp.zeros((SEL_ROWS - ns, TQ), BF16)], axis=0)
    for r in range(R):
        qTr = qT_ref[0, r * dh:(r + 1) * dh, :]
        qsel_ref[:, r * TQ:(r + 1) * TQ] = jnp.concatenate(
            [qTr, sel_rows, arows_ref[0, r], pad_sel], axis=0)

    k_io = lax.broadcasted_iota(jnp.int32, (KC, R * TQ), 0)
    q_io = lax.broadcasted_iota(jnp.int32, (KC, R * TQ), 1) & (TQ - 1)
    causal = k_io <= q_io
    band = k_io > q_io

    def reset():
        m_ref[...] = jnp.full(m_ref.shape, NEG_INF, F32)
        acc_ref[...] = jnp.zeros(acc_ref.shape, F32)

    def accumulate(k_tile, vT_tile, q_aug, mask):
        s = _dot(k_tile, q_aug)
        if mask is not None:
            s = jnp.where(mask, s, NEG_INF)
        m_old = m_ref[...]
        m_new = jnp.maximum(m_old, jnp.max(s, axis=0, keepdims=True))
        p = jnp.exp(s - m_new).astype(BF16)
        acc_ref[...] = jnp.exp(m_old - m_new) * acc_ref[...] + _dot(vT_tile, p)
        m_ref[...] = m_new

    def normalized():
        acc = acc_ref[...]
        return acc[:dh, :] / acc[dh:dh + 1, :]

    reset()

    def sel_step(c, carry):
        start = pl.multiple_of(c * KC, KC)
        accumulate(ksel_ref[0, 0, pl.ds(start, KC), :], vselT_ref[0, 0, c], qsel_ref[...], None)
        return carry

    lax.fori_loop(0, qi, sel_step, 0)
    accumulate(ksel_ref[0, 0, pl.ds(pl.multiple_of(q0, KC), KC), :], vselT_ref[0, 0, qi],
               qsel_ref[...], causal)
    osel_ref[...] = normalized()

    reset()
    accumulate(kwin_ref[0, 0, pl.ds(pl.multiple_of(q0, KC), KC), :], vwinT_ref[0, 0, qi],
               qwin_ref[...], causal)

    @pl.when(qi >= 1)
    def _():
        start = pl.multiple_of(q0 - KC, KC)
        accumulate(kwin_ref[0, 0, pl.ds(start, KC), :], vwinT_ref[0, 0, qi - 1],
                   qwin_ref[...], None)

    @pl.when(qi >= 2)
    def _():
        start = pl.multiple_of(q0 - 2 * KC, KC)
        accumulate(kwin_ref[0, 0, pl.ds(start, KC), :], vwinT_ref[0, 0, qi - 2],
                   qwin_ref[...], band)

    o_win = normalized()

    for r in range(R):
        lanes = slice(r * TQ, (r + 1) * TQ)
        o = (gT_ref[0, r:r + 1, :] * o_cmp[r]
             + gT_ref[0, R + r:R + r + 1, :] * osel_ref[:, lanes]
             + gT_ref[0, 2 * R + r:2 * R + r + 1, :] * o_win[:, lanes])
        oT_ref[0, r * dh:(r + 1) * dh, :] = o.astype(BF16)


def _attention(qT, gT, kc, vcT, ksel, vselT, kwin, vwinT, arows, ovT):
    B, HD, S = qT.shape
    G, R, dh = N_KV_GROUPS, HEADS_PER_GROUP, HEAD_DIM
    nc = kc.shape[2]
    ns = S // SLC_LEN
    n_kc = S // KC
    return pl.pallas_call(
        functools.partial(_attn_kernel, ns=ns, nc=nc, top_n=min(SLC_TOP_N, ns)),
        grid=(B, G, S // TQ),
        in_specs=[
            pl.BlockSpec((1, R * dh, TQ), lambda b, g, i: (b, g, i)),
            pl.BlockSpec((1, gT.shape[1] // G, TQ), lambda b, g, i: (b, g, i)),
            pl.BlockSpec((1, 1, nc, QWIN_ROWS), lambda b, g, i: (b, g, 0, 0)),
            pl.BlockSpec((1, 1, dh, nc), lambda b, g, i: (b, g, 0, 0)),
            pl.BlockSpec((1, 1, S, QSEL_ROWS), lambda b, g, i: (b, g, 0, 0)),
            pl.BlockSpec((1, 1, n_kc, V_ROWS, KC), lambda b, g, i: (b, g, 0, 0, 0)),
            pl.BlockSpec((1, 1, S, QWIN_ROWS), lambda b, g, i: (b, g, 0, 0)),
            pl.BlockSpec((1, 1, n_kc, V_ROWS, KC), lambda b, g, i: (b, g, 0, 0, 0)),
            pl.BlockSpec((1, R, ALIBI_ROWS, TQ), lambda b, g, i: (g, 0, 0, 0)),
            pl.BlockSpec(ovT.shape, lambda b, g, i: (0, 0)),
        ],
        out_specs=pl.BlockSpec((1, R * dh, TQ), lambda b, g, i: (b, g, i)),
        out_shape=jax.ShapeDtypeStruct((B, HD, S), BF16),
        scratch_shapes=[
            pltpu.VMEM((QSEL_ROWS, R * TQ), BF16),
            pltpu.VMEM((QWIN_ROWS, R * TQ), BF16),
            pltpu.VMEM((1, R * TQ), F32),
            pltpu.VMEM((V_ROWS, R * TQ), F32),
            pltpu.VMEM((dh, R * TQ), F32),
        ],
        compiler_params=_params(("parallel", "parallel", "arbitrary")),
        name="nsa_attention",
    )(qT, gT, kc, vcT, ksel, vselT, kwin, vwinT, arows, ovT)


def _oproj_kernel(x_ref, oT_ref, mod_ref, ng_ref, w_ref, o_ref):
    y = _dot_tn(oT_ref[0], w_ref[...])
    o_ref[0] = _post(x_ref[0], y, mod_ref, ng_ref, 1, 1.0)


def _oproj(x, oT, mod, ng, w_out):
    B, S, D = x.shape
    tm = min(TOKEN_TILE, S)
    return pl.pallas_call(
        _oproj_kernel,
        grid=(B, S // tm),
        in_specs=[
            pl.BlockSpec((1, tm, D), lambda b, i: (b, i, 0)),
            pl.BlockSpec((1, oT.shape[1], tm), lambda b, i: (b, 0, i)),
            pl.BlockSpec((1, 9, D), lambda b, i: (b, 0, 0)),
            pl.BlockSpec((6, D), lambda b, i: (0, 0)),
            _resident(w_out.shape),
        ],
        out_specs=pl.BlockSpec((1, tm, D), lambda b, i: (b, i, 0)),
        out_shape=jax.ShapeDtypeStruct(x.shape, F32),
        compiler_params=_params(("parallel", "parallel")),
        name="oproj",
    )(x, oT, mod, ng, w_out)


def _split3(v):
    hi = v.astype(BF16)
    r1 = v - hi.astype(F32)
    mid = r1.astype(BF16)
    lo = (r1 - mid.astype(F32)).astype(BF16)
    return hi, mid, lo


def _alibi_rows():
    h = jnp.arange(1, N_HEADS + 1, dtype=F32)
    slopes = jnp.exp2(-8.0 * h / N_HEADS).reshape(N_KV_GROUPS, HEADS_PER_GROUP)
    parts = _split3(slopes)
    rows = jnp.stack(list(parts) * 2 + [jnp.zeros_like(parts[0])] * (ALIBI_ROWS - 6), axis=-1)
    return jnp.broadcast_to(rows[..., None], rows.shape + (TQ,))


def _overlap_t(ns, nc, n_cmp):
    i = jnp.arange(nc)[None, :]
    j = jnp.arange(ns)[:, None]
    start = i * CMP_STRIDE
    ov = (start < (j + 1) * SLC_LEN) & (start + CMP_LEN - 1 >= j * SLC_LEN) & (i < n_cmp)
    return ov.astype(BF16)


def _pad_cols(w, width):
    return jnp.pad(w, ((0, 0), (0, width - w.shape[1])))


def kernel(x, c, ada_w, ada_b, norm_g, ffn_w_in, ffn_w_out, a_w_in, a_conv, a_w_out,
           kv_norm_g, kv_ada_w, kv_ada_b, kv_w, cmp_pos, cmp_w1, cmp_w2, b_w_in, b_w_out):
    B, S, D = x.shape
    G, R, dh = N_KV_GROUPS, HEADS_PER_GROUP, HEAD_DIM
    n_a = DEPTH // 2
    assert S % TQ == 0 and WINDOW == 2 * KC and TQ == KC
    assert D == N_HEADS * dh

    mods = _ada(c, ada_w, ada_b, tn=2304).reshape(DEPTH, B, 9, D)
    kvmod = _ada(c, kv_ada_w[None], kv_ada_b[None], tn=1024).reshape(B, 2, D)

    arows = _alibi_rows()
    nc = S // CMP_STRIDE
    n_cmp = (S - CMP_LEN) // CMP_STRIDE + 1
    ovT = _overlap_t(S // SLC_LEN, nc, n_cmp)

    kv_state = None
    for layer in range(DEPTH):
        mod = mods[layer]
        ng = norm_g[layer].reshape(6, D)
        if layer == n_a:
            gcols = G * dh
            w0 = kv_w[:, :2 * gcols].astype(BF16)
            k_sel = kv_w[:, 2 * gcols:3 * gcols]
            k_win = kv_w[:, 4 * gcols:5 * gcols]
            wks = jnp.concatenate(
                [_pad_cols(k_sel[:, g * dh:(g + 1) * dh], QSEL_ROWS) for g in range(G)], axis=1)
            wkw = jnp.concatenate(
                [_pad_cols(k_win[:, g * dh:(g + 1) * dh], QWIN_ROWS) for g in range(G)], axis=1)
            wvsT = kv_w[:, 3 * gcols:4 * gcols].T.astype(BF16)
            wvwT = kv_w[:, 5 * gcols:6 * gcols].T.astype(BF16)
            kv0, ksel, kwin, vselT, vwinT = _build_kv(
                x, kvmod, kv_norm_g, w0, wks.astype(BF16), wkw.astype(BF16), wvsT, wvwT)

            def chunked(a):
                a = a.reshape(B, S, G, dh).transpose(0, 2, 1, 3)
                return a.reshape(B, G, nc, CMP_STRIDE * dh)

            half = CMP_STRIDE * dh
            kc, vcT = _compress(
                chunked(kv0[..., :gcols]), chunked(kv0[..., gcols:]),
                cmp_pos.reshape(4, half),
                cmp_w1.reshape(2, 2, half, cmp_w1.shape[-1]).astype(BF16),
                _pad_cols(cmp_w2[0], QWIN_ROWS).astype(BF16),
                cmp_w2[1].T.astype(BF16))
            kv_state = (kc, vcT, ksel, vselT, kwin, vwinT)

        x = _ffn(x, mod, ng, ffn_w_in[layer, 0].astype(BF16), ffn_w_out[layer, 0].astype(BF16), 0)
        if layer < n_a:
            x = _conv_mixer(x, mod, ng, a_w_in[layer].astype(BF16), a_conv[layer],
                            a_w_out[layer].astype(BF16))
        else:
            w_in = b_w_in[layer - n_a]
            wqT = w_in[:, :N_HEADS * dh].T.astype(BF16)
            wg = w_in[:, N_HEADS * dh:].reshape(D, N_BRANCH, G, R)
            wg = jnp.pad(wg.transpose(2, 1, 3, 0), ((0, 0), (0, 1), (0, 0), (0, 0)))
            wgT = wg.reshape(G * (N_BRANCH + 1) * R, D).astype(BF16)
            qT, gT = _qproj(x, mod, ng, wqT, wgT)
            kc, vcT, ksel, vselT, kwin, vwinT = kv_state
            oT = _attention(qT, gT, kc, vcT, ksel, vselT, kwin, vwinT, arows, ovT)
            x = _oproj(x, oT, mod, ng, b_w_out[layer - n_a].astype(BF16))
        x = _ffn(x, mod, ng, ffn_w_in[layer, 1].astype(BF16), ffn_w_out[layer, 1].astype(BF16), 2)
    return x
```

```python
import functools

import jax
import jax.numpy as jnp
from jax import lax
from jax.experimental import pallas as pl
from jax.experimental.pallas import tpu as pltpu

F32 = jnp.float32
BF16 = jnp.bfloat16

N_HEADS = 16
N_KV_GROUPS = 4
HEADS_PER_GROUP = N_HEADS // N_KV_GROUPS
HEAD_DIM = 64
N_BRANCH = 3
CMP_LEN = 32
CMP_STRIDE = 16
SLC_LEN = 64
SLC_SHIFT = 6
SLC_TOP_N = 16
WINDOW = 512
CONV_WIDTH = 3
RMS_EPS = 1e-6
NEG_INF = -1e30
FORCE_SCORE = 1e30
DEPTH = 4

TQ = 256
KC = 256
SEL_ROWS = 64
ALIBI_ROWS = 16
QSEL_ROWS = 256
QWIN_ROWS = 128
V_ROWS = HEAD_DIM + 16
MASK_BIG = float(2 ** 30)
LOG2E = 1.4426950408889634

VMEM_LIMIT = 56 * 1024 * 1024
TOKEN_TILE = 512


def _params(sem):
    return pltpu.CompilerParams(dimension_semantics=sem, vmem_limit_bytes=VMEM_LIMIT)


def _resident(shape):
    nd = len(shape)
    return pl.BlockSpec(shape, lambda *_: (0,) * nd, pipeline_mode=pl.Buffered(1))


def _rms(x, g):
    ms = jnp.mean(x * x, axis=-1, keepdims=True)
    return x * lax.rsqrt(ms + RMS_EPS) * g


def _pre(x, mod_ref, ng_ref, sub):
    shift = mod_ref[0, 3 * sub:3 * sub + 1, :]
    scale = mod_ref[0, 3 * sub + 1:3 * sub + 2, :]
    return _rms(x, ng_ref[2 * sub:2 * sub + 1, :]) * (1.0 + scale) + shift


def _post(x, y, mod_ref, ng_ref, sub, weight):
    gate = mod_ref[0, 3 * sub + 2:3 * sub + 3, :]
    return x + weight * gate * _rms(y, ng_ref[2 * sub + 1:2 * sub + 2, :])


def _dot(a, b):
    return jnp.dot(a, b, preferred_element_type=F32)


def _dot_nt(a, b):
    return lax.dot_general(a, b, (((1,), (1,)), ((), ())), preferred_element_type=F32)


def _dot_tn(a, b):
    return lax.dot_general(a, b, (((0,), (0,)), ((), ())), preferred_element_type=F32)


def _ada_kernel(c_ref, w_ref, b_ref, o_ref):
    c = c_ref[...]
    sc = c * jax.nn.sigmoid(c)
    o_ref[0] = _dot(sc, w_ref[0]) + b_ref[0]


def _ada(c, w, b, tn):
    L, D, N = w.shape
    B = c.shape[0]
    return pl.pallas_call(
        _ada_kernel,
        grid=(L, N // tn),
        in_specs=[
            pl.BlockSpec((B, D), lambda l, j: (0, 0)),
            pl.BlockSpec((1, D, tn), lambda l, j: (l, 0, j)),
            pl.BlockSpec((1, 1, tn), lambda l, j: (l, 0, j)),
        ],
        out_specs=pl.BlockSpec((1, B, tn), lambda l, j: (l, 0, j)),
        out_shape=jax.ShapeDtypeStruct((L, B, N), F32),
        compiler_params=_params(("parallel", "parallel")),
        name="ada",
    )(c, w, b.reshape(L, 1, N))


def _ffn_kernel(x_ref, mod_ref, ng_ref, win_ref, wout_ref, o_ref, *, sub, d_ff, n_chunks):
    x = x_ref[0]
    h = _pre(x, mod_ref, ng_ref, sub).astype(BF16)
    tf = d_ff // n_chunks
    y = None
    for c in range(n_chunks):
        g = _dot(h, win_ref[:, c * tf:(c + 1) * tf])
        u = _dot(h, win_ref[:, d_ff + c * tf:d_ff + (c + 1) * tf])
        a = (g * jax.nn.sigmoid(g) * u).astype(BF16)
        part = _dot(a, wout_ref[c * tf:(c + 1) * tf, :])
        y = part if y is None else y + part
    o_ref[0] = _post(x, y, mod_ref, ng_ref, sub, 0.5)


def _ffn(x, mod, ng, w_in, w_out, sub):
    B, S, D = x.shape
    d_ff = w_out.shape[0]
    tm = min(TOKEN_TILE, S)
    n_chunks = 2 if (d_ff // 2) % 128 == 0 else 1
    return pl.pallas_call(
        functools.partial(_ffn_kernel, sub=sub, d_ff=d_ff, n_chunks=n_chunks),
        grid=(B, S // tm),
        in_specs=[
            pl.BlockSpec((1, tm, D), lambda b, i: (b, i, 0)),
            pl.BlockSpec((1, 9, D), lambda b, i: (b, 0, 0)),
            pl.BlockSpec((6, D), lambda b, i: (0, 0)),
            _resident(w_in.shape),
            _resident(w_out.shape),
        ],
        out_specs=pl.BlockSpec((1, tm, D), lambda b, i: (b, i, 0)),
        out_shape=jax.ShapeDtypeStruct(x.shape, F32),
        compiler_params=_params(("parallel", "parallel")),
        name="ffn",
    )(x, mod, ng, w_in, w_out)


def _conv_kernel(x_ref, mod_ref, ng_ref, win_ref, cw_ref, wout_ref, o_ref, vbuf, *, tm):
    D = x_ref.shape[-1]
    x = x_ref[0]
    h = _pre(x, mod_ref, ng_ref, 1).astype(BF16)
    proj = _dot(h, win_ref[...])
    b_gate = proj[:, :D]
    v = proj[:, D:2 * D] * proj[:, 2 * D:]

    @pl.when(pl.program_id(1) == 0)
    def _():
        vbuf[0:8, :] = jnp.zeros((8, D), F32)

    vbuf[8:8 + tm, :] = v
    y = (cw_ref[2:3, :] * v
         + cw_ref[1:2, :] * vbuf[7:7 + tm, :]
         + cw_ref[0:1, :] * vbuf[6:6 + tm, :])
    vbuf[0:8, :] = v[tm - 8:, :]
    z = (b_gate * y).astype(BF16)
    o_ref[0] = _post(x, _dot(z, wout_ref[...]), mod_ref, ng_ref, 1, 1.0)


def _conv_mixer(x, mod, ng, w_in, conv_w, w_out):
    B, S, D = x.shape
    tm = min(TOKEN_TILE, S)
    return pl.pallas_call(
        functools.partial(_conv_kernel, tm=tm),
        grid=(B, S // tm),
        in_specs=[
            pl.BlockSpec((1, tm, D), lambda b, i: (b, i, 0)),
            pl.BlockSpec((1, 9, D), lambda b, i: (b, 0, 0)),
            pl.BlockSpec((6, D), lambda b, i: (0, 0)),
            _resident(w_in.shape),
            pl.BlockSpec(conv_w.shape, lambda b, i: (0, 0)),
            _resident(w_out.shape),
        ],
        out_specs=pl.BlockSpec((1, tm, D), lambda b, i: (b, i, 0)),
        out_shape=jax.ShapeDtypeStruct(x.shape, F32),
        scratch_shapes=[pltpu.VMEM((tm + 8, D), F32)],
        compiler_params=_params(("parallel", "arbitrary")),
        name="conv_mixer",
    )(x, mod, ng, w_in, conv_w, w_out)


def _pos_cols(shape, first, hi, lo):
    lane = lax.broadcasted_iota(jnp.int32, shape, 1)
    hi_b = jnp.broadcast_to(hi, shape)
    lo_b = jnp.broadcast_to(lo, shape)
    return jnp.where((lane >= first) & (lane < first + 3), hi_b,
                     jnp.where((lane >= first + 3) & (lane < first + 6), lo_b, 0.0))


def _kv_kernel(x_ref, mod_ref, g_ref, w0_ref, wks_ref, wkw_ref, wvsT_ref, wvwT_ref,
               kv0_ref, ksel_ref, kwin_ref, vselT_ref, vwinT_ref, *, tm):
    G = N_KV_GROUPS
    x = x_ref[0]
    h = (_rms(x, g_ref[...]) * (1.0 + mod_ref[0, 1:2, :]) + mod_ref[0, 0:1, :]).astype(BF16)
    kv0_ref[0] = _dot(h, w0_ref[...])

    tok = pl.program_id(1) * tm + lax.broadcasted_iota(jnp.int32, (tm, 1), 0)
    blk = tok >> SLC_SHIFT
    blk64 = (blk * SLC_LEN).astype(F32)
    off = (tok & (SLC_LEN - 1)).astype(F32)
    lane = lax.broadcasted_iota(jnp.int32, (tm, QSEL_ROWS), 1)
    onehot = ((lane - HEAD_DIM) == blk).astype(F32)
    sel_const = onehot + _pos_cols((tm, QSEL_ROWS), HEAD_DIM + SEL_ROWS, blk64, off)
    win_const = _pos_cols((tm, QWIN_ROWS), HEAD_DIM, blk64, off)

    ks = _dot(h, wks_ref[...])
    kw = _dot(h, wkw_ref[...])
    for g in range(G):
        ksel_ref[0, g] = (ks[:, g * QSEL_ROWS:(g + 1) * QSEL_ROWS] + sel_const).astype(BF16)
        kwin_ref[0, g] = (kw[:, g * QWIN_ROWS:(g + 1) * QWIN_ROWS] + win_const).astype(BF16)

    ones_rows = (lax.broadcasted_iota(jnp.int32, (V_ROWS - HEAD_DIM, KC), 0) == 0).astype(F32)
    for wT_ref, out_ref in ((wvsT_ref, vselT_ref), (wvwT_ref, vwinT_ref)):
        vT = _dot_nt(wT_ref[...], h)
        for g in range(G):
            for c in range(tm // KC):
                blk_v = vT[g * HEAD_DIM:(g + 1) * HEAD_DIM, c * KC:(c + 1) * KC]
                out_ref[0, g, c] = jnp.concatenate([blk_v, ones_rows], axis=0).astype(BF16)


def _build_kv(x, kvmod, kv_norm_g, w0, wks, wkw, wvsT, wvwT):
    B, S, D = x.shape
    G = N_KV_GROUPS
    tm = min(TOKEN_TILE, S)
    out_shape = (
        jax.ShapeDtypeStruct((B, S, w0.shape[1]), F32),
        jax.ShapeDtypeStruct((B, G, S, QSEL_ROWS), BF16),
        jax.ShapeDtypeStruct((B, G, S, QWIN_ROWS), BF16),
        jax.ShapeDtypeStruct((B, G, S // KC, V_ROWS, KC), BF16),
        jax.ShapeDtypeStruct((B, G, S // KC, V_ROWS, KC), BF16),
    )
    return pl.pallas_call(
        functools.partial(_kv_kernel, tm=tm),
        grid=(B, S // tm),
        in_specs=[
            pl.BlockSpec((1, tm, D), lambda b, i: (b, i, 0)),
            pl.BlockSpec((1, 2, D), lambda b, i: (b, 0, 0)),
            pl.BlockSpec((1, D), lambda b, i: (0, 0)),
            _resident(w0.shape), _resident(wks.shape), _resident(wkw.shape),
            _resident(wvsT.shape), _resident(wvwT.shape),
        ],
        out_specs=(
            pl.BlockSpec((1, tm, w0.shape[1]), lambda b, i: (b, i, 0)),
            pl.BlockSpec((1, G, tm, QSEL_ROWS), lambda b, i: (b, 0, i, 0)),
            pl.BlockSpec((1, G, tm, QWIN_ROWS), lambda b, i: (b, 0, i, 0)),
            pl.BlockSpec((1, G, tm // KC, V_ROWS, KC), lambda b, i: (b, 0, i, 0, 0)),
            pl.BlockSpec((1, G, tm // KC, V_ROWS, KC), lambda b, i: (b, 0, i, 0, 0)),
        ),
        out_shape=out_shape,
        compiler_params=_params(("parallel", "parallel")),
        name="build_kv",
    )(x, kvmod, kv_norm_g.reshape(1, D), w0, wks, wkw, wvsT, wvwT)


def _gelu_tanh(x):
    return 0.5 * x * (1.0 + jnp.tanh(0.7978845608028654 * (x + 0.044715 * (x * x * x))))


def _cmp_kernel(ck_ref, cv_ref, pos_ref, w1_ref, w2k_ref, w2vT_ref, kc_ref, vcT_ref, *, nc):
    def hidden(c, kv):
        a = _dot((c + pos_ref[2 * kv:2 * kv + 1, :]).astype(BF16), w1_ref[kv, 0])
        b = _dot((c + pos_ref[2 * kv + 1:2 * kv + 2, :]).astype(BF16), w1_ref[kv, 1])
        return _gelu_tanh(a + pltpu.roll(b, nc - 1, 0))

    hk = hidden(ck_ref[0, 0], 0).astype(BF16)
    hv = hidden(cv_ref[0, 0], 1).astype(BF16)
    n = lax.broadcasted_iota(jnp.int32, (nc, 1), 0)
    end_hi = (CMP_STRIDE * (n + 1)).astype(F32)
    end_lo = jnp.full((nc, 1), float(CMP_LEN - 1 - CMP_STRIDE), F32)
    kc = _dot(hk, w2k_ref[...]) + _pos_cols((nc, QWIN_ROWS), HEAD_DIM, end_hi, end_lo)
    kc_ref[0, 0] = kc.astype(BF16)
    vcT_ref[0, 0] = _dot_nt(w2vT_ref[...], hv).astype(BF16)


def _compress(ck, cv, pos, w1, w2k, w2vT):
    B, G, nc, F = ck.shape
    return pl.pallas_call(
        functools.partial(_cmp_kernel, nc=nc),
        grid=(B, G),
        in_specs=[
            pl.BlockSpec((1, 1, nc, F), lambda b, g: (b, g, 0, 0)),
            pl.BlockSpec((1, 1, nc, F), lambda b, g: (b, g, 0, 0)),
            pl.BlockSpec(pos.shape, lambda b, g: (0, 0)),
            _resident(w1.shape), _resident(w2k.shape), _resident(w2vT.shape),
        ],
        out_specs=(
            pl.BlockSpec((1, 1, nc, QWIN_ROWS), lambda b, g: (b, g, 0, 0)),
            pl.BlockSpec((1, 1, HEAD_DIM, nc), lambda b, g: (b, g, 0, 0)),
        ),
        out_shape=(
            jax.ShapeDtypeStruct((B, G, nc, QWIN_ROWS), BF16),
            jax.ShapeDtypeStruct((B, G, HEAD_DIM, nc), BF16),
        ),
        compiler_params=_params(("parallel", "parallel")),
        name="compress_kv",
    )(ck, cv, pos, w1, w2k, w2vT)


def _qproj_kernel(x_ref, mod_ref, ng_ref, wqT_ref, wgT_ref, qT_ref, gT_ref):
    h = _pre(x_ref[0], mod_ref, ng_ref, 1).astype(BF16)
    qT_ref[0] = (_dot_nt(wqT_ref[...], h) * (HEAD_DIM ** -0.5 * LOG2E)).astype(BF16)
    gT_ref[0] = jax.nn.sigmoid(_dot_nt(wgT_ref[...], h))


def _qproj(x, mod, ng, wqT, wgT):
    B, S, D = x.shape
    tm = min(TOKEN_TILE, S)
    return pl.pallas_call(
        _qproj_kernel,
        grid=(B, S // tm),
        in_specs=[
            pl.BlockSpec((1, tm, D), lambda b, i: (b, i, 0)),
            pl.BlockSpec((1, 9, D), lambda b, i: (b, 0, 0)),
            pl.BlockSpec((6, D), lambda b, i: (0, 0)),
            _resident(wqT.shape), _resident(wgT.shape),
        ],
        out_specs=(
            pl.BlockSpec((1, wqT.shape[0], tm), lambda b, i: (b, 0, i)),
            pl.BlockSpec((1, wgT.shape[0], tm), lambda b, i: (b, 0, i)),
        ),
        out_shape=(
            jax.ShapeDtypeStruct((B, wqT.shape[0], S), BF16),
            jax.ShapeDtypeStruct((B, wgT.shape[0], S), F32),
        ),
        compiler_params=_params(("parallel", "parallel")),
        name="qproj",
    )(x, mod, ng, wqT, wgT)


def _attn_kernel(qT_ref, gT_ref, kc_ref, vcT_ref, ksel_ref, vselT_ref, kwin_ref, vwinT_ref,
                 arows_ref, ovT_ref, oT_ref, qsel_ref, qwin_ref, m_ref, acc_ref, ocmp_ref, owin_ref,
                 s_buf, p_buf, a_buf, *, ns, nc, top_n):
    R = HEADS_PER_GROUP
    dh = HEAD_DIM
    qi = pl.program_id(2)
    q0 = qi * TQ
    t_row = q0 + lax.broadcasted_iota(jnp.int32, (1, TQ), 1)

    def lanes(r):
        return slice(r * TQ, (r + 1) * TQ)

    pad_win = jnp.zeros((QWIN_ROWS - dh - ALIBI_ROWS, TQ), BF16)
    pad_sel = jnp.zeros((QSEL_ROWS - dh - SEL_ROWS - ALIBI_ROWS, TQ), BF16)
    for r in range(R):
        qwin_ref[:, lanes(r)] = jnp.concatenate(
            [qT_ref[0, r * dh:(r + 1) * dh, :], arows_ref[0, r], pad_win], axis=0)

    n_io = lax.broadcasted_iota(jnp.int32, (nc, R * TQ), 0)
    t_all = q0 + (lax.broadcasted_iota(jnp.int32, (nc, R * TQ), 1) & (TQ - 1))
    valid_c = (CMP_STRIDE * n_io + (CMP_LEN - 1)) <= t_all
    s = jnp.where(valid_c, _dot(kc_ref[0, 0], qwin_ref[...]), NEG_INF)
    e = jnp.where(valid_c, jnp.exp2(s - jnp.max(s, axis=0, keepdims=True)), 0.0)
    p = e / jnp.maximum(jnp.sum(e, axis=0, keepdims=True), 1e-30)
    o_cmp = _dot(vcT_ref[0, 0], p.astype(BF16))
    psum = p[:, lanes(0)]
    for r in range(1, R):
        psum = psum + p[:, lanes(r)]

    p_hi = psum.astype(BF16)
    rem = psum - p_hi.astype(F32)
    p_mid = rem.astype(BF16)
    p_lo = (rem - p_mid.astype(F32)).astype(BF16)
    ovT = ovT_ref[...]
    imp = _dot(ovT, p_hi) + _dot(ovT, p_mid) + _dot(ovT, p_lo)
    j_io = lax.broadcasted_iota(jnp.int32, (ns, TQ), 0)
    cur = t_row >> SLC_SHIFT
    forced = (j_io == 0) | (j_io == cur) | (j_io == cur - 1)
    future = (j_io * SLC_LEN) > t_row
    imp = jnp.where(forced, FORCE_SCORE, jnp.where(future, NEG_INF, imp))

    ocmp_ref[...] = o_cmp

    k_io = lax.broadcasted_iota(jnp.int32, (KC, R * TQ), 0)
    q_io = lax.broadcasted_iota(jnp.int32, (KC, R * TQ), 1) & (TQ - 1)
    causal = k_io <= q_io
    band = k_io > q_io

    def win_scores(c):
        start = pl.multiple_of(c * KC, KC)
        return _dot(kwin_ref[0, 0, pl.ds(start, KC), :], qwin_ref[...])

    c_mid = jnp.maximum(qi - 1, 0)
    c_far = jnp.maximum(qi - 2, 0)
    s_near = jnp.where(causal, win_scores(qi), NEG_INF)
    s_mid = jnp.where(qi >= 1, win_scores(c_mid), NEG_INF)
    s_far = jnp.where(band & (qi >= 2), win_scores(c_far), NEG_INF)
    m_win = jnp.maximum(jnp.max(s_near, axis=0, keepdims=True),
                        jnp.maximum(jnp.max(s_mid, axis=0, keepdims=True),
                                    jnp.max(s_far, axis=0, keepdims=True)))
    acc_win = (_dot(vwinT_ref[0, 0, qi], jnp.exp2(s_near - m_win).astype(BF16))
               + _dot(vwinT_ref[0, 0, c_mid], jnp.exp2(s_mid - m_win).astype(BF16))
               + _dot(vwinT_ref[0, 0, c_far], jnp.exp2(s_far - m_win).astype(BF16)))
    owin_ref[...] = acc_win[:dh, :] / acc_win[dh:dh + 1, :]

    groups = [imp[8 * k:8 * k + 8, :] for k in range(ns // 8)]
    ranks = [jnp.zeros((8, TQ), jnp.int32) for _ in groups]
    sub_io = lax.broadcasted_iota(jnp.int32, (8, TQ), 0)
    for jp in range(ns):
        row = imp[jp:jp + 1, :]
        for k, grp in enumerate(groups):
            if 8 * k > jp:
                beats = (row >= grp).astype(jnp.int32)
            elif 8 * k + 7 < jp:
                beats = (row > grp).astype(jnp.int32)
            else:
                beats = jnp.where(sub_io > jp - 8 * k, (row >= grp).astype(jnp.int32),
                                  (row > grp).astype(jnp.int32))
            ranks[k] = ranks[k] + beats
    rank = jnp.concatenate(ranks, axis=0)
    sel_rows = jnp.where(rank < top_n, 0.0, -MASK_BIG).astype(BF16)
    if ns < SEL_ROWS:
        sel_rows = jnp.concatenate([sel_rows, jnp.zeros((SEL_ROWS - ns, TQ), BF16)], axis=0)
    for r in range(R):
        qsel_ref[:, lanes(r)] = jnp.concatenate(
            [qT_ref[0, r * dh:(r + 1) * dh, :], sel_rows, arows_ref[0, r], pad_sel], axis=0)

    def sel_scores(c):
        start = pl.multiple_of(c * KC, KC)
        return _dot(ksel_ref[0, 0, pl.ds(start, KC), :], qsel_ref[...])

    def sel_softmax(slot, mask):
        sc = s_buf[slot]
        if mask is not None:
            sc = jnp.where(mask, sc, NEG_INF)
        m_old = m_ref[...]
        m_new = jnp.maximum(m_old, jnp.max(sc, axis=0, keepdims=True))
        p_buf[slot] = jnp.exp2(sc - m_new).astype(BF16)
        a_buf[slot] = jnp.exp2(m_old - m_new)
        m_ref[...] = m_new

    def sel_values(c, slot):
        acc_ref[...] = a_buf[slot] * acc_ref[...] + _dot(vselT_ref[0, 0, c], p_buf[slot])

    m_ref[...] = jnp.full(m_ref.shape, NEG_INF, F32)
    acc_ref[...] = jnp.zeros(acc_ref.shape, F32)
    p_buf[1] = jnp.zeros(p_buf.shape[1:], BF16)
    a_buf[1] = jnp.ones(a_buf.shape[1:], F32)
    s_buf[0] = sel_scores(0)

    def sel_pair(i, carry):
        c = 2 * i
        s_buf[1] = sel_scores(c + 1)
        sel_softmax(0, None)
        sel_values(jnp.maximum(c - 1, 0), 1)
        s_buf[0] = sel_scores(c + 2)
        sel_softmax(1, None)
        sel_values(c, 0)
        return carry

    lax.fori_loop(0, qi >> 1, sel_pair, 0)

    @pl.when((qi & 1) == 1)
    def _():
        sel_softmax(0, None)
        s_buf[1] = sel_scores(qi)
        sel_values(jnp.maximum(qi - 2, 0), 1)
        sel_softmax(1, causal)
        sel_values(qi - 1, 0)
        sel_values(qi, 1)

    @pl.when((qi & 1) == 0)
    def _():
        sel_softmax(0, causal)
        sel_values(jnp.maximum(qi - 1, 0), 1)
        sel_values(qi, 0)

    acc = acc_ref[...]
    o_sel = acc[:dh, :] / acc[dh:dh + 1, :]
    for r in range(R):
        o = (gT_ref[0, r:r + 1, :] * ocmp_ref[:, lanes(r)]
             + gT_ref[0, R + r:R + r + 1, :] * o_sel[:, lanes(r)]
             + gT_ref[0, 2 * R + r:2 * R + r + 1, :] * owin_ref[:, lanes(r)])
        oT_ref[0, r * dh:(r + 1) * dh, :] = o.astype(BF16)


def _attention(qT, gT, kc, vcT, ksel, vselT, kwin, vwinT, arows, ovT):
    B, HD, S = qT.shape
    G, R, dh = N_KV_GROUPS, HEADS_PER_GROUP, HEAD_DIM
    nc = kc.shape[2]
    ns = S // SLC_LEN
    n_kc = S // KC
    return pl.pallas_call(
        functools.partial(_attn_kernel, ns=ns, nc=nc, top_n=min(SLC_TOP_N, ns)),
        grid=(B, G, S // TQ),
        in_specs=[
            pl.BlockSpec((1, R * dh, TQ), lambda b, g, i: (b, g, i)),
            pl.BlockSpec((1, gT.shape[1] // G, TQ), lambda b, g, i: (b, g, i)),
            pl.BlockSpec((1, 1, nc, QWIN_ROWS), lambda b, g, i: (b, g, 0, 0)),
            pl.BlockSpec((1, 1, dh, nc), lambda b, g, i: (b, g, 0, 0)),
            pl.BlockSpec((1, 1, S, QSEL_ROWS), lambda b, g, i: (b, g, 0, 0)),
            pl.BlockSpec((1, 1, n_kc, V_ROWS, KC), lambda b, g, i: (b, g, 0, 0, 0)),
            pl.BlockSpec((1, 1, S, QWIN_ROWS), lambda b, g, i: (b, g, 0, 0)),
            pl.BlockSpec((1, 1, n_kc, V_ROWS, KC), lambda b, g, i: (b, g, 0, 0, 0)),
            pl.BlockSpec((1, R, ALIBI_ROWS, TQ), lambda b, g, i: (g, 0, 0, 0)),
            pl.BlockSpec(ovT.shape, lambda b, g, i: (0, 0)),
        ],
        out_specs=pl.BlockSpec((1, R * dh, TQ), lambda b, g, i: (b, g, i)),
        out_shape=jax.ShapeDtypeStruct((B, HD, S), BF16),
        scratch_shapes=[
            pltpu.VMEM((QSEL_ROWS, R * TQ), BF16),
            pltpu.VMEM((QWIN_ROWS, R * TQ), BF16),
            pltpu.VMEM((1, R * TQ), F32),
            pltpu.VMEM((V_ROWS, R * TQ), F32),
            pltpu.VMEM((dh, R * TQ), F32),
            pltpu.VMEM((dh, R * TQ), F32),
            pltpu.VMEM((2, KC, R * TQ), F32),
            pltpu.VMEM((2, KC, R * TQ), BF16),
            pltpu.VMEM((2, 1, R * TQ), F32),
        ],
        compiler_params=_params(("parallel", "parallel", "arbitrary")),
        name="nsa_attention",
    )(qT, gT, kc, vcT, ksel, vselT, kwin, vwinT, arows, ovT)


def _oproj_kernel(x_ref, oT_ref, mod_ref, ng_ref, w_ref, o_ref):
    y = _dot_tn(oT_ref[0], w_ref[...])
    o_ref[0] = _post(x_ref[0], y, mod_ref, ng_ref, 1, 1.0)


def _oproj(x, oT, mod, ng, w_out):
    B, S, D = x.shape
    tm = min(TOKEN_TILE, S)
    return pl.pallas_call(
        _oproj_kernel,
        grid=(B, S // tm),
        in_specs=[
            pl.BlockSpec((1, tm, D), lambda b, i: (b, i, 0)),
            pl.BlockSpec((1, oT.shape[1], tm), lambda b, i: (b, 0, i)),
            pl.BlockSpec((1, 9, D), lambda b, i: (b, 0, 0)),
            pl.BlockSpec((6, D), lambda b, i: (0, 0)),
            _resident(w_out.shape),
        ],
        out_specs=pl.BlockSpec((1, tm, D), lambda b, i: (b, i, 0)),
        out_shape=jax.ShapeDtypeStruct(x.shape, F32),
        compiler_params=_params(("parallel", "parallel")),
        name="oproj",
    )(x, oT, mod, ng, w_out)


def _split3(v):
    hi = v.astype(BF16)
    r1 = v - hi.astype(F32)
    mid = r1.astype(BF16)
    lo = (r1 - mid.astype(F32)).astype(BF16)
    return hi, mid, lo


def _alibi_rows():
    h = jnp.arange(1, N_HEADS + 1, dtype=F32)
    slopes = jnp.exp2(-8.0 * h / N_HEADS).reshape(N_KV_GROUPS, HEADS_PER_GROUP)
    parts = _split3(slopes * LOG2E)
    rows = jnp.stack(list(parts) * 2 + [jnp.zeros_like(parts[0])] * (ALIBI_ROWS - 6), axis=-1)
    return jnp.broadcast_to(rows[..., None], rows.shape + (TQ,))


def _overlap_t(ns, nc, n_cmp):
    i = jnp.arange(nc)[None, :]
    j = jnp.arange(ns)[:, None]
    start = i * CMP_STRIDE
    ov = (start < (j + 1) * SLC_LEN) & (start + CMP_LEN - 1 >= j * SLC_LEN) & (i < n_cmp)
    return ov.astype(BF16)


def _pad_cols(w, width):
    return jnp.pad(w, ((0, 0), (0, width - w.shape[1])))


def kernel(x, c, ada_w, ada_b, norm_g, ffn_w_in, ffn_w_out, a_w_in, a_conv, a_w_out,
           kv_norm_g, kv_ada_w, kv_ada_b, kv_w, cmp_pos, cmp_w1, cmp_w2, b_w_in, b_w_out):
    B, S, D = x.shape
    G, R, dh = N_KV_GROUPS, HEADS_PER_GROUP, HEAD_DIM
    n_a = DEPTH // 2
    assert S % TQ == 0 and WINDOW == 2 * KC and TQ == KC
    assert D == N_HEADS * dh

    mods = _ada(c, ada_w, ada_b, tn=2304).reshape(DEPTH, B, 9, D)
    kvmod = _ada(c, kv_ada_w[None], kv_ada_b[None], tn=1024).reshape(B, 2, D)

    arows = _alibi_rows()
    nc = S // CMP_STRIDE
    n_cmp = (S - CMP_LEN) // CMP_STRIDE + 1
    ovT = _overlap_t(S // SLC_LEN, nc, n_cmp)

    kv_state = None
    for layer in range(DEPTH):
        mod = mods[layer]
        ng = norm_g[layer].reshape(6, D)
        if layer == n_a:
            gcols = G * dh
            w0 = kv_w[:, :2 * gcols].astype(BF16)
            k_sel = kv_w[:, 2 * gcols:3 * gcols]
            k_win = kv_w[:, 4 * gcols:5 * gcols]
            wks = jnp.concatenate(
                [_pad_cols(k_sel[:, g * dh:(g + 1) * dh], QSEL_ROWS) for g in range(G)], axis=1)
            wkw = jnp.concatenate(
                [_pad_cols(k_win[:, g * dh:(g + 1) * dh], QWIN_ROWS) for g in range(G)], axis=1)
            wvsT = kv_w[:, 3 * gcols:4 * gcols].T.astype(BF16)
            wvwT = kv_w[:, 5 * gcols:6 * gcols].T.astype(BF16)
            kv0, ksel, kwin, vselT, vwinT = _build_kv(
                x, kvmod, kv_norm_g, w0, wks.astype(BF16), wkw.astype(BF16), wvsT, wvwT)

            def chunked(a):
                a = a.reshape(B, S, G, dh).transpose(0, 2, 1, 3)
                return a.reshape(B, G, nc, CMP_STRIDE * dh)

            half = CMP_STRIDE * dh
            kc, vcT = _compress(
                chunked(kv0[..., :gcols]), chunked(kv0[..., gcols:]),
                cmp_pos.reshape(4, half),
                cmp_w1.reshape(2, 2, half, cmp_w1.shape[-1]).astype(BF16),
                _pad_cols(cmp_w2[0], QWIN_ROWS).astype(BF16),
                cmp_w2[1].T.astype(BF16))
            kv_state = (kc, vcT, ksel, vselT, kwin, vwinT)

        x = _ffn(x, mod, ng, ffn_w_in[layer, 0].astype(BF16), ffn_w_out[layer, 0].astype(BF16), 0)
        if layer < n_a:
            x = _conv_mixer(x, mod, ng, a_w_in[layer].astype(BF16), a_conv[layer],
                            a_w_out[layer].astype(BF16))
        else:
            w_in = b_w_in[layer - n_a]
            wqT = w_in[:, :N_HEADS * dh].T.astype(BF16)
            wg = w_in[:, N_HEADS * dh:].reshape(D, N_BRANCH, G, R)
            wg = jnp.pad(wg.transpose(2, 1, 3, 0), ((0, 0), (0, 1), (0, 0), (0, 0)))
            wgT = wg.reshape(G * (N_BRANCH + 1) * R, D).astype(BF16)
            qT, gT = _qproj(x, mod, ng, wqT, wgT)
            kc, vcT, ksel, vselT, kwin, vwinT = kv_state
            oT = _attention(qT, gT, kc, vcT, ksel, vselT, kwin, vwinT, arows, ovT)
            x = _oproj(x, oT, mod, ng, b_w_out[layer - n_a].astype(BF16))
        x = _ffn(x, mod, ng, ffn_w_in[layer, 1].astype(BF16), ffn_w_out[layer, 1].astype(BF16), 2)
    return x
```

```python
import functools

import jax
import jax.numpy as jnp
from jax import lax
from jax.experimental import pallas as pl
from jax.experimental.pallas import tpu as pltpu

F32 = jnp.float32
BF16 = jnp.bfloat16

N_HEADS = 16
N_KV_GROUPS = 4
HEADS_PER_GROUP = N_HEADS // N_KV_GROUPS
HEAD_DIM = 64
N_BRANCH = 3
CMP_LEN = 32
CMP_STRIDE = 16
SLC_LEN = 64
SLC_SHIFT = 6
SLC_TOP_N = 16
WINDOW = 512
CONV_WIDTH = 3
RMS_EPS = 1e-6
NEG_INF = -1e30
FORCE_SCORE = 1e30
DEPTH = 4

TQ = 256
KC = 256
SEL_ROWS = 64
ALIBI_ROWS = 16
QSEL_ROWS = 256
QWIN_ROWS = 128
V_ROWS = HEAD_DIM + 16
MASK_BIG = float(2 ** 30)
LOG2E = 1.4426950408889634

MXU_COLS = 256
VMEM_LIMIT = 56 * 1024 * 1024
TOKEN_TILE = 512
FFN_TOKEN_TILE = 1024
FFN_ROW_SPLITS = 2


def _params(sem):
    return pltpu.CompilerParams(dimension_semantics=sem, vmem_limit_bytes=VMEM_LIMIT)


def _resident(shape):
    nd = len(shape)
    return pl.BlockSpec(shape, lambda *_: (0,) * nd, pipeline_mode=pl.Buffered(1))


def _rms(x, g):
    ms = jnp.mean(x * x, axis=-1, keepdims=True)
    return x * lax.rsqrt(ms + RMS_EPS) * g


def _pre(x, mod_ref, ng_ref, sub):
    shift = mod_ref[0, 3 * sub:3 * sub + 1, :]
    scale = mod_ref[0, 3 * sub + 1:3 * sub + 2, :]
    return _rms(x, ng_ref[2 * sub:2 * sub + 1, :]) * (1.0 + scale) + shift


def _post(x, y, mod_ref, ng_ref, sub, weight):
    gate = mod_ref[0, 3 * sub + 2:3 * sub + 3, :]
    return x + weight * gate * _rms(y, ng_ref[2 * sub + 1:2 * sub + 2, :])


def _dot(a, b):
    return jnp.dot(a, b, preferred_element_type=F32)


def _dot_nt(a, b):
    return lax.dot_general(a, b, (((1,), (1,)), ((), ())), preferred_element_type=F32)


def _dot_tn(a, b):
    return lax.dot_general(a, b, (((0,), (0,)), ((), ())), preferred_element_type=F32)


def _ada_kernel(c_ref, w_ref, b_ref, o_ref):
    c = c_ref[...]
    sc = c * jax.nn.sigmoid(c)
    o_ref[0] = _dot(sc, w_ref[0]) + b_ref[0]


def _ada(c, w, b, tn):
    L, D, N = w.shape
    B = c.shape[0]
    return pl.pallas_call(
        _ada_kernel,
        grid=(L, N // tn),
        in_specs=[
            pl.BlockSpec((B, D), lambda l, j: (0, 0)),
            pl.BlockSpec((1, D, tn), lambda l, j: (l, 0, j)),
            pl.BlockSpec((1, 1, tn), lambda l, j: (l, 0, j)),
        ],
        out_specs=pl.BlockSpec((1, B, tn), lambda l, j: (l, 0, j)),
        out_shape=jax.ShapeDtypeStruct((L, B, N), F32),
        compiler_params=_params(("parallel", "parallel")),
        name="ada",
    )(c, w, b.reshape(L, 1, N))


def _ffn_kernel(x_ref, mod_ref, ng_ref, win_ref, wout_ref, o_ref, *, sub, d_ff, n_chunks):
    tf = d_ff // n_chunks
    rows = x_ref.shape[1] // FFN_ROW_SPLITS
    for i in range(FFN_ROW_SPLITS):
        x = x_ref[0, i * rows:(i + 1) * rows, :]
        h = _pre(x, mod_ref, ng_ref, sub).astype(BF16)
        y = None
        for c in range(n_chunks):
            g = _dot(h, win_ref[:, c * tf:(c + 1) * tf])
            u = _dot(h, win_ref[:, d_ff + c * tf:d_ff + (c + 1) * tf])
            a = (g * jax.nn.sigmoid(g) * u).astype(BF16)
            part = _dot(a, wout_ref[c * tf:(c + 1) * tf, :])
            y = part if y is None else y + part
        o_ref[0, i * rows:(i + 1) * rows, :] = _post(x, y, mod_ref, ng_ref, sub, 0.5)


def _ffn(x, mod, ng, w_in, w_out, sub):
    B, S, D = x.shape
    d_ff = w_out.shape[0]
    tm = min(FFN_TOKEN_TILE, S)
    n_chunks = next(n for n in (1, 2, 4) if (d_ff // n) % MXU_COLS == 0 and d_ff % n == 0)
    return pl.pallas_call(
        functools.partial(_ffn_kernel, sub=sub, d_ff=d_ff, n_chunks=n_chunks),
        grid=(B, S // tm),
        in_specs=[
            pl.BlockSpec((1, tm, D), lambda b, i: (b, i, 0)),
            pl.BlockSpec((1, 9, D), lambda b, i: (b, 0, 0)),
            pl.BlockSpec((6, D), lambda b, i: (0, 0)),
            _resident(w_in.shape),
            _resident(w_out.shape),
        ],
        out_specs=pl.BlockSpec((1, tm, D), lambda b, i: (b, i, 0)),
        out_shape=jax.ShapeDtypeStruct(x.shape, F32),
        compiler_params=_params(("parallel", "parallel")),
        name="ffn",
    )(x, mod, ng, w_in, w_out)


def _conv_kernel(x_ref, mod_ref, ng_ref, win_ref, cw_ref, wout_ref, o_ref, vbuf, *, tm):
    D = x_ref.shape[-1]
    x = x_ref[0]
    h = _pre(x, mod_ref, ng_ref, 1).astype(BF16)
    proj = _dot(h, win_ref[...])
    b_gate = proj[:, :D]
    v = proj[:, D:2 * D] * proj[:, 2 * D:]

    @pl.when(pl.program_id(1) == 0)
    def _():
        vbuf[0:8, :] = jnp.zeros((8, D), F32)

    vbuf[8:8 + tm, :] = v
    y = (cw_ref[2:3, :] * v
         + cw_ref[1:2, :] * vbuf[7:7 + tm, :]
         + cw_ref[0:1, :] * vbuf[6:6 + tm, :])
    vbuf[0:8, :] = v[tm - 8:, :]
    z = (b_gate * y).astype(BF16)
    o_ref[0] = _post(x, _dot(z, wout_ref[...]), mod_ref, ng_ref, 1, 1.0)


def _conv_mixer(x, mod, ng, w_in, conv_w, w_out):
    B, S, D = x.shape
    tm = min(TOKEN_TILE, S)
    return pl.pallas_call(
        functools.partial(_conv_kernel, tm=tm),
        grid=(B, S // tm),
        in_specs=[
            pl.BlockSpec((1, tm, D), lambda b, i: (b, i, 0)),
            pl.BlockSpec((1, 9, D), lambda b, i: (b, 0, 0)),
            pl.BlockSpec((6, D), lambda b, i: (0, 0)),
            _resident(w_in.shape),
            pl.BlockSpec(conv_w.shape, lambda b, i: (0, 0)),
            _resident(w_out.shape),
        ],
        out_specs=pl.BlockSpec((1, tm, D), lambda b, i: (b, i, 0)),
        out_shape=jax.ShapeDtypeStruct(x.shape, F32),
        scratch_shapes=[pltpu.VMEM((tm + 8, D), F32)],
        compiler_params=_params(("parallel", "arbitrary")),
        name="conv_mixer",
    )(x, mod, ng, w_in, conv_w, w_out)


def _pos_cols(shape, first, hi, lo):
    lane = lax.broadcasted_iota(jnp.int32, shape, 1)
    hi_b = jnp.broadcast_to(hi, shape)
    lo_b = jnp.broadcast_to(lo, shape)
    return jnp.where((lane >= first) & (lane < first + 3), hi_b,
                     jnp.where((lane >= first + 3) & (lane < first + 6), lo_b, 0.0))


def _kv_kernel(x_ref, mod_ref, g_ref, w0_ref, wks_ref, wkw_ref, wvsT_ref, wvwT_ref,
               kv0_ref, ksel_ref, kwin_ref, vselT_ref, vwinT_ref, *, tm):
    G = N_KV_GROUPS
    x = x_ref[0]
    h = (_rms(x, g_ref[...]) * (1.0 + mod_ref[0, 1:2, :]) + mod_ref[0, 0:1, :]).astype(BF16)
    kv0_ref[0] = _dot(h, w0_ref[...])

    tok = pl.program_id(1) * tm + lax.broadcasted_iota(jnp.int32, (tm, 1), 0)
    blk = tok >> SLC_SHIFT
    blk64 = (blk * SLC_LEN).astype(F32)
    off = (tok & (SLC_LEN - 1)).astype(F32)
    lane = lax.broadcasted_iota(jnp.int32, (tm, QSEL_ROWS), 1)
    onehot = ((lane - HEAD_DIM) == blk).astype(F32)
    sel_const = onehot + _pos_cols((tm, QSEL_ROWS), HEAD_DIM + SEL_ROWS, blk64, off)
    win_const = _pos_cols((tm, QWIN_ROWS), HEAD_DIM, blk64, off)

    ks = _dot(h, wks_ref[...])
    kw = _dot(h, wkw_ref[...])
    for g in range(G):
        ksel_ref[0, g] = (ks[:, g * QSEL_ROWS:(g + 1) * QSEL_ROWS] + sel_const).astype(BF16)
        kwin_ref[0, g] = (kw[:, g * QWIN_ROWS:(g + 1) * QWIN_ROWS] + win_const).astype(BF16)

    ones_rows = (lax.broadcasted_iota(jnp.int32, (V_ROWS - HEAD_DIM, KC), 0) == 0).astype(F32)
    for wT_ref, out_ref in ((wvsT_ref, vselT_ref), (wvwT_ref, vwinT_ref)):
        vT = _dot_nt(wT_ref[...], h)
        for g in range(G):
            for c in range(tm // KC):
                blk_v = vT[g * HEAD_DIM:(g + 1) * HEAD_DIM, c * KC:(c + 1) * KC]
                out_ref[0, g, c] = jnp.concatenate([blk_v, ones_rows], axis=0).astype(BF16)


def _build_kv(x, kvmod, kv_norm_g, w0, wks, wkw, wvsT, wvwT):
    B, S, D = x.shape
    G = N_KV_GROUPS
    tm = min(TOKEN_TILE, S)
    out_shape = (
        jax.ShapeDtypeStruct((B, S, w0.shape[1]), F32),
        jax.ShapeDtypeStruct((B, G, S, QSEL_ROWS), BF16),
        jax.ShapeDtypeStruct((B, G, S, QWIN_ROWS), BF16),
        jax.ShapeDtypeStruct((B, G, S // KC, V_ROWS, KC), BF16),
        jax.ShapeDtypeStruct((B, G, S // KC, V_ROWS, KC), BF16),
    )
    return pl.pallas_call(
        functools.partial(_kv_kernel, tm=tm),
        grid=(B, S // tm),
        in_specs=[
            pl.BlockSpec((1, tm, D), lambda b, i: (b, i, 0)),
            pl.BlockSpec((1, 2, D), lambda b, i: (b, 0, 0)),
            pl.BlockSpec((1, D), lambda b, i: (0, 0)),
            _resident(w0.shape), _resident(wks.shape), _resident(wkw.shape),
            _resident(wvsT.shape), _resident(wvwT.shape),
        ],
        out_specs=(
            pl.BlockSpec((1, tm, w0.shape[1]), lambda b, i: (b, i, 0)),
            pl.BlockSpec((1, G, tm, QSEL_ROWS), lambda b, i: (b, 0, i, 0)),
            pl.BlockSpec((1, G, tm, QWIN_ROWS), lambda b, i: (b, 0, i, 0)),
            pl.BlockSpec((1, G, tm // KC, V_ROWS, KC), lambda b, i: (b, 0, i, 0, 0)),
            pl.BlockSpec((1, G, tm // KC, V_ROWS, KC), lambda b, i: (b, 0, i, 0, 0)),
        ),
        out_shape=out_shape,
        compiler_params=_params(("parallel", "parallel")),
        name="build_kv",
    )(x, kvmod, kv_norm_g.reshape(1, D), w0, wks, wkw, wvsT, wvwT)


def _gelu_tanh(x):
    return 0.5 * x * (1.0 + jnp.tanh(0.7978845608028654 * (x + 0.044715 * (x * x * x))))


def _cmp_kernel(ck_ref, cv_ref, pos_ref, w1_ref, w2k_ref, w2vT_ref, kc_ref, vcT_ref, *, nc):
    def hidden(c, kv):
        a = _dot((c + pos_ref[2 * kv:2 * kv + 1, :]).astype(BF16), w1_ref[kv, 0])
        b = _dot((c + pos_ref[2 * kv + 1:2 * kv + 2, :]).astype(BF16), w1_ref[kv, 1])
        return _gelu_tanh(a + pltpu.roll(b, nc - 1, 0))

    hk = hidden(ck_ref[0, 0], 0).astype(BF16)
    hv = hidden(cv_ref[0, 0], 1).astype(BF16)
    n = lax.broadcasted_iota(jnp.int32, (nc, 1), 0)
    end_hi = (CMP_STRIDE * (n + 1)).astype(F32)
    end_lo = jnp.full((nc, 1), float(CMP_LEN - 1 - CMP_STRIDE), F32)
    kc = _dot(hk, w2k_ref[...]) + _pos_cols((nc, QWIN_ROWS), HEAD_DIM, end_hi, end_lo)
    kc_ref[0, 0] = kc.astype(BF16)
    vcT_ref[0, 0] = _dot_nt(w2vT_ref[...], hv).astype(BF16)


def _compress(ck, cv, pos, w1, w2k, w2vT):
    B, G, nc, F = ck.shape
    return pl.pallas_call(
        functools.partial(_cmp_kernel, nc=nc),
        grid=(B, G),
        in_specs=[
            pl.BlockSpec((1, 1, nc, F), lambda b, g: (b, g, 0, 0)),
            pl.BlockSpec((1, 1, nc, F), lambda b, g: (b, g, 0, 0)),
            pl.BlockSpec(pos.shape, lambda b, g: (0, 0)),
            _resident(w1.shape), _resident(w2k.shape), _resident(w2vT.shape),
        ],
        out_specs=(
            pl.BlockSpec((1, 1, nc, QWIN_ROWS), lambda b, g: (b, g, 0, 0)),
            pl.BlockSpec((1, 1, HEAD_DIM, nc), lambda b, g: (b, g, 0, 0)),
        ),
        out_shape=(
            jax.ShapeDtypeStruct((B, G, nc, QWIN_ROWS), BF16),
            jax.ShapeDtypeStruct((B, G, HEAD_DIM, nc), BF16),
        ),
        compiler_params=_params(("parallel", "parallel")),
        name="compress_kv",
    )(ck, cv, pos, w1, w2k, w2vT)


def _qproj_kernel(x_ref, mod_ref, ng_ref, wqT_ref, wgT_ref, qT_ref, gT_ref):
    h = _pre(x_ref[0], mod_ref, ng_ref, 1).astype(BF16)
    qT_ref[0] = (_dot_nt(wqT_ref[...], h) * (HEAD_DIM ** -0.5 * LOG2E)).astype(BF16)
    gT_ref[0] = jax.nn.sigmoid(_dot_nt(wgT_ref[...], h))


def _qproj(x, mod, ng, wqT, wgT):
    B, S, D = x.shape
    tm = min(TOKEN_TILE, S)
    return pl.pallas_call(
        _qproj_kernel,
        grid=(B, S // tm),
        in_specs=[
            pl.BlockSpec((1, tm, D), lambda b, i: (b, i, 0)),
            pl.BlockSpec((1, 9, D), lambda b, i: (b, 0, 0)),
            pl.BlockSpec((6, D), lambda b, i: (0, 0)),
            _resident(wqT.shape), _resident(wgT.shape),
        ],
        out_specs=(
            pl.BlockSpec((1, wqT.shape[0], tm), lambda b, i: (b, 0, i)),
            pl.BlockSpec((1, wgT.shape[0], tm), lambda b, i: (b, 0, i)),
        ),
        out_shape=(
            jax.ShapeDtypeStruct((B, wqT.shape[0], S), BF16),
            jax.ShapeDtypeStruct((B, wgT.shape[0], S), F32),
        ),
        compiler_params=_params(("parallel", "parallel")),
        name="qproj",
    )(x, mod, ng, wqT, wgT)


def _attn_kernel(qT_ref, gT_ref, kc_ref, vcT_ref, ksel_ref, vselT_ref, kwin_ref, vwinT_ref,
                 arows_ref, ovT_ref, oT_ref, qsel_ref, qwin_ref, m_ref, acc_ref, ocmp_ref, owin_ref,
                 s_buf, p_buf, a_buf, *, ns, nc, top_n):
    R = HEADS_PER_GROUP
    dh = HEAD_DIM
    qi = pl.program_id(2)
    q0 = qi * TQ
    t_row = q0 + lax.broadcasted_iota(jnp.int32, (1, TQ), 1)

    def lanes(r):
        return slice(r * TQ, (r + 1) * TQ)

    pad_win = jnp.zeros((QWIN_ROWS - dh - ALIBI_ROWS, TQ), BF16)
    pad_sel = jnp.zeros((QSEL_ROWS - dh - SEL_ROWS - ALIBI_ROWS, TQ), BF16)
    for r in range(R):
        qwin_ref[:, lanes(r)] = jnp.concatenate(
            [qT_ref[0, r * dh:(r + 1) * dh, :], arows_ref[0, r], pad_win], axis=0)

    n_io = lax.broadcasted_iota(jnp.int32, (nc, R * TQ), 0)
    t_all = q0 + (lax.broadcasted_iota(jnp.int32, (nc, R * TQ), 1) & (TQ - 1))
    valid_c = (CMP_STRIDE * n_io + (CMP_LEN - 1)) <= t_all
    s = jnp.where(valid_c, _dot(kc_ref[0, 0], qwin_ref[...]), NEG_INF)
    e = jnp.where(valid_c, jnp.exp2(s - jnp.max(s, axis=0, keepdims=True)), 0.0)
    p = e / jnp.maximum(jnp.sum(e, axis=0, keepdims=True), 1e-30)
    o_cmp = _dot(vcT_ref[0, 0], p.astype(BF16))
    psum = p[:, lanes(0)]
    for r in range(1, R):
        psum = psum + p[:, lanes(r)]

    p_hi = psum.astype(BF16)
    rem = psum - p_hi.astype(F32)
    p_mid = rem.astype(BF16)
    p_lo = (rem - p_mid.astype(F32)).astype(BF16)
    ovT = ovT_ref[...]
    imp = _dot(ovT, p_hi) + _dot(ovT, p_mid) + _dot(ovT, p_lo)
    j_io = lax.broadcasted_iota(jnp.int32, (ns, TQ), 0)
    cur = t_row >> SLC_SHIFT
    forced = (j_io == 0) | (j_io == cur) | (j_io == cur - 1)
    future = (j_io * SLC_LEN) > t_row
    imp = jnp.where(forced, FORCE_SCORE, jnp.where(future, NEG_INF, imp))

    ocmp_ref[...] = o_cmp

    k_io = lax.broadcasted_iota(jnp.int32, (KC, R * TQ), 0)
    q_io = lax.broadcasted_iota(jnp.int32, (KC, R * TQ), 1) & (TQ - 1)
    causal = k_io <= q_io
    band = k_io > q_io

    def win_scores(c):
        start = pl.multiple_of(c * KC, KC)
        return _dot(kwin_ref[0, 0, pl.ds(start, KC), :], qwin_ref[...])

    c_mid = jnp.maximum(qi - 1, 0)
    c_far = jnp.maximum(qi - 2, 0)
    s_near = jnp.where(causal, win_scores(qi), NEG_INF)
    s_mid = jnp.where(qi >= 1, win_scores(c_mid), NEG_INF)
    s_far = jnp.where(band & (qi >= 2), win_scores(c_far), NEG_INF)
    m_win = jnp.maximum(jnp.max(s_near, axis=0, keepdims=True),
                        jnp.maximum(jnp.max(s_mid, axis=0, keepdims=True),
                                    jnp.max(s_far, axis=0, keepdims=True)))
    acc_win = (_dot(vwinT_ref[0, 0, qi], jnp.exp2(s_near - m_win).astype(BF16))
               + _dot(vwinT_ref[0, 0, c_mid], jnp.exp2(s_mid - m_win).astype(BF16))
               + _dot(vwinT_ref[0, 0, c_far], jnp.exp2(s_far - m_win).astype(BF16)))
    owin_ref[...] = acc_win[:dh, :] / acc_win[dh:dh + 1, :]

    groups = [imp[8 * k:8 * k + 8, :] for k in range(ns // 8)]
    ranks = [jnp.zeros((8, TQ), jnp.int32) for _ in groups]
    sub_io = lax.broadcasted_iota(jnp.int32, (8, TQ), 0)
    for jp in range(ns):
        row = imp[jp:jp + 1, :]
        for k, grp in enumerate(groups):
            if 8 * k > jp:
                beats = (row >= grp).astype(jnp.int32)
            elif 8 * k + 7 < jp:
                beats = (row > grp).astype(jnp.int32)
            else:
                beats = jnp.where(sub_io > jp - 8 * k, (row >= grp).astype(jnp.int32),
                                  (row > grp).astype(jnp.int32))
            ranks[k] = ranks[k] + beats
    rank = jnp.concatenate(ranks, axis=0)
    sel_rows = jnp.where(rank < top_n, 0.0, -MASK_BIG).astype(BF16)
    if ns < SEL_ROWS:
        sel_rows = jnp.concatenate([sel_rows, jnp.zeros((SEL_ROWS - ns, TQ), BF16)], axis=0)
    for r in range(R):
        qsel_ref[:, lanes(r)] = jnp.concatenate(
            [qT_ref[0, r * dh:(r + 1) * dh, :], sel_rows, arows_ref[0, r], pad_sel], axis=0)

    def sel_scores(c):
        start = pl.multiple_of(c * KC, KC)
        return _dot(ksel_ref[0, 0, pl.ds(start, KC), :], qsel_ref[...])

    def sel_softmax(slot, mask):
        sc = s_buf[slot]
        if mask is not None:
            sc = jnp.where(mask, sc, NEG_INF)
        m_old = m_ref[...]
        m_new = jnp.maximum(m_old, jnp.max(sc, axis=0, keepdims=True))
        p_buf[slot] = jnp.exp2(sc - m_new).astype(BF16)
        a_buf[slot] = jnp.exp2(m_old - m_new)
        m_ref[...] = m_new

    def sel_values(c, slot):
        acc_ref[...] = a_buf[slot] * acc_ref[...] + _dot(vselT_ref[0, 0, c], p_buf[slot])

    m_ref[...] = jnp.full(m_ref.shape, NEG_INF, F32)
    acc_ref[...] = jnp.zeros(acc_ref.shape, F32)
    p_buf[1] = jnp.zeros(p_buf.shape[1:], BF16)
    a_buf[1] = jnp.ones(a_buf.shape[1:], F32)
    s_buf[0] = sel_scores(0)

    def sel_pair(i, carry):
        c = 2 * i
        s_buf[1] = sel_scores(c + 1)
        sel_softmax(0, None)
        sel_values(jnp.maximum(c - 1, 0), 1)
        s_buf[0] = sel_scores(c + 2)
        sel_softmax(1, None)
        sel_values(c, 0)
        return carry

    lax.fori_loop(0, qi >> 1, sel_pair, 0)

    @pl.when((qi & 1) == 1)
    def _():
        sel_softmax(0, None)
        s_buf[1] = sel_scores(qi)
        sel_values(jnp.maximum(qi - 2, 0), 1)
        sel_softmax(1, causal)
        sel_values(qi - 1, 0)
        sel_values(qi, 1)

    @pl.when((qi & 1) == 0)
    def _():
        sel_softmax(0, causal)
        sel_values(jnp.maximum(qi - 1, 0), 1)
        sel_values(qi, 0)

    acc = acc_ref[...]
    o_sel = acc[:dh, :] / acc[dh:dh + 1, :]
    for r in range(R):
        o = (gT_ref[0, r:r + 1, :] * ocmp_ref[:, lanes(r)]
             + gT_ref[0, R + r:R + r + 1, :] * o_sel[:, lanes(r)]
             + gT_ref[0, 2 * R + r:2 * R + r + 1, :] * owin_ref[:, lanes(r)])
        oT_ref[0, r * dh:(r + 1) * dh, :] = o.astype(BF16)


def _attention(qT, gT, kc, vcT, ksel, vselT, kwin, vwinT, arows, ovT):
    B, HD, S = qT.shape
    G, R, dh = N_KV_GROUPS, HEADS_PER_GROUP, HEAD_DIM
    nc = kc.shape[2]
    ns = S // SLC_LEN
    n_kc = S // KC
    return pl.pallas_call(
        functools.partial(_attn_kernel, ns=ns, nc=nc, top_n=min(SLC_TOP_N, ns)),
        grid=(B, G, S // TQ),
        in_specs=[
            pl.BlockSpec((1, R * dh, TQ), lambda b, g, i: (b, g, i)),
            pl.BlockSpec((1, gT.shape[1] // G, TQ), lambda b, g, i: (b, g, i)),
            pl.BlockSpec((1, 1, nc, QWIN_ROWS), lambda b, g, i: (b, g, 0, 0)),
            pl.BlockSpec((1, 1, dh, nc), lambda b, g, i: (b, g, 0, 0)),
            pl.BlockSpec((1, 1, S, QSEL_ROWS), lambda b, g, i: (b, g, 0, 0)),
            pl.BlockSpec((1, 1, n_kc, V_ROWS, KC), lambda b, g, i: (b, g, 0, 0, 0)),
            pl.BlockSpec((1, 1, S, QWIN_ROWS), lambda b, g, i: (b, g, 0, 0)),
            pl.BlockSpec((1, 1, n_kc, V_ROWS, KC), lambda b, g, i: (b, g, 0, 0, 0)),
            pl.BlockSpec((1, R, ALIBI_ROWS, TQ), lambda b, g, i: (g, 0, 0, 0)),
            pl.BlockSpec(ovT.shape, lambda b, g, i: (0, 0)),
        ],
        out_specs=pl.BlockSpec((1, R * dh, TQ), lambda b, g, i: (b, g, i)),
        out_shape=jax.ShapeDtypeStruct((B, HD, S), BF16),
        scratch_shapes=[
            pltpu.VMEM((QSEL_ROWS, R * TQ), BF16),
            pltpu.VMEM((QWIN_ROWS, R * TQ), BF16),
            pltpu.VMEM((1, R * TQ), F32),
            pltpu.VMEM((V_ROWS, R * TQ), F32),
            pltpu.VMEM((dh, R * TQ), F32),
            pltpu.VMEM((dh, R * TQ), F32),
            pltpu.VMEM((2, KC, R * TQ), F32),
            pltpu.VMEM((2, KC, R * TQ), BF16),
            pltpu.VMEM((2, 1, R * TQ), F32),
        ],
        compiler_params=_params(("parallel", "parallel", "arbitrary")),
        name="nsa_attention",
    )(qT, gT, kc, vcT, ksel, vselT, kwin, vwinT, arows, ovT)


def _oproj_kernel(x_ref, oT_ref, mod_ref, ng_ref, w_ref, o_ref):
    y = _dot_tn(oT_ref[0], w_ref[...])
    o_ref[0] = _post(x_ref[0], y, mod_ref, ng_ref, 1, 1.0)


def _oproj(x, oT, mod, ng, w_out):
    B, S, D = x.shape
    tm = min(TOKEN_TILE, S)
    return pl.pallas_call(
        _oproj_kernel,
        grid=(B, S // tm),
        in_specs=[
            pl.BlockSpec((1, tm, D), lambda b, i: (b, i, 0)),
            pl.BlockSpec((1, oT.shape[1], tm), lambda b, i: (b, 0, i)),
            pl.BlockSpec((1, 9, D), lambda b, i: (b, 0, 0)),
            pl.BlockSpec((6, D), lambda b, i: (0, 0)),
            _resident(w_out.shape),
        ],
        out_specs=pl.BlockSpec((1, tm, D), lambda b, i: (b, i, 0)),
        out_shape=jax.ShapeDtypeStruct(x.shape, F32),
        compiler_params=_params(("parallel", "parallel")),
        name="oproj",
    )(x, oT, mod, ng, w_out)


def _split3(v):
    hi = v.astype(BF16)
    r1 = v - hi.astype(F32)
    mid = r1.astype(BF16)
    lo = (r1 - mid.astype(F32)).astype(BF16)
    return hi, mid, lo


def _alibi_rows():
    h = jnp.arange(1, N_HEADS + 1, dtype=F32)
    slopes = jnp.exp2(-8.0 * h / N_HEADS).reshape(N_KV_GROUPS, HEADS_PER_GROUP)
    parts = _split3(slopes * LOG2E)
    rows = jnp.stack(list(parts) * 2 + [jnp.zeros_like(parts[0])] * (ALIBI_ROWS - 6), axis=-1)
    return jnp.broadcast_to(rows[..., None], rows.shape + (TQ,))


def _overlap_t(ns, nc, n_cmp):
    i = jnp.arange(nc)[None, :]
    j = jnp.arange(ns)[:, None]
    start = i * CMP_STRIDE
    ov = (start < (j + 1) * SLC_LEN) & (start + CMP_LEN - 1 >= j * SLC_LEN) & (i < n_cmp)
    return ov.astype(BF16)


def _pad_cols(w, width):
    return jnp.pad(w, ((0, 0), (0, width - w.shape[1])))


def kernel(x, c, ada_w, ada_b, norm_g, ffn_w_in, ffn_w_out, a_w_in, a_conv, a_w_out,
           kv_norm_g, kv_ada_w, kv_ada_b, kv_w, cmp_pos, cmp_w1, cmp_w2, b_w_in, b_w_out):
    B, S, D = x.shape
    G, R, dh = N_KV_GROUPS, HEADS_PER_GROUP, HEAD_DIM
    n_a = DEPTH // 2
    assert S % TQ == 0 and WINDOW == 2 * KC and TQ == KC
    assert D == N_HEADS * dh

    mods = _ada(c, ada_w, ada_b, tn=2304).reshape(DEPTH, B, 9, D)
    kvmod = _ada(c, kv_ada_w[None], kv_ada_b[None], tn=1024).reshape(B, 2, D)

    arows = _alibi_rows()
    nc = S // CMP_STRIDE
    n_cmp = (S - CMP_LEN) // CMP_STRIDE + 1
    ovT = _overlap_t(S // SLC_LEN, nc, n_cmp)

    kv_state = None
    for layer in range(DEPTH):
        mod = mods[layer]
        ng = norm_g[layer].reshape(6, D)
        if layer == n_a:
            gcols = G * dh
            w0 = kv_w[:, :2 * gcols].astype(BF16)
            k_sel = kv_w[:, 2 * gcols:3 * gcols]
            k_win = kv_w[:, 4 * gcols:5 * gcols]
            wks = jnp.concatenate(
                [_pad_cols(k_sel[:, g * dh:(g + 1) * dh], QSEL_ROWS) for g in range(G)], axis=1)
            wkw = jnp.concatenate(
                [_pad_cols(k_win[:, g * dh:(g + 1) * dh], QWIN_ROWS) for g in range(G)], axis=1)
            wvsT = kv_w[:, 3 * gcols:4 * gcols].T.astype(BF16)
            wvwT = kv_w[:, 5 * gcols:6 * gcols].T.astype(BF16)
            kv0, ksel, kwin, vselT, vwinT = _build_kv(
                x, kvmod, kv_norm_g, w0, wks.astype(BF16), wkw.astype(BF16), wvsT, wvwT)

            def chunked(a):
                a = a.reshape(B, S, G, dh).transpose(0, 2, 1, 3)
                return a.reshape(B, G, nc, CMP_STRIDE * dh)

            half = CMP_STRIDE * dh
            kc, vcT = _compress(
                chunked(kv0[..., :gcols]), chunked(kv0[..., gcols:]),
                cmp_pos.reshape(4, half),
                cmp_w1.reshape(2, 2, half, cmp_w1.shape[-1]).astype(BF16),
                _pad_cols(cmp_w2[0], QWIN_ROWS).astype(BF16),
                cmp_w2[1].T.astype(BF16))
            kv_state = (kc, vcT, ksel, vselT, kwin, vwinT)

        x = _ffn(x, mod, ng, ffn_w_in[layer, 0].astype(BF16), ffn_w_out[layer, 0].astype(BF16), 0)
        if layer < n_a:
            x = _conv_mixer(x, mod, ng, a_w_in[layer].astype(BF16), a_conv[layer],
                            a_w_out[layer].astype(BF16))
        else:
            w_in = b_w_in[layer - n_a]
            wqT = w_in[:, :N_HEADS * dh].T.astype(BF16)
            wg = w_in[:, N_HEADS * dh:].reshape(D, N_BRANCH, G, R)
            wg = jnp.pad(wg.transpose(2, 1, 3, 0), ((0, 0), (0, 1), (0, 0), (0, 0)))
            wgT = wg.reshape(G * (N_BRANCH + 1) * R, D).astype(BF16)
            qT, gT = _qproj(x, mod, ng, wqT, wgT)
            kc, vcT, ksel, vselT, kwin, vwinT = kv_state
            oT = _attention(qT, gT, kc, vcT, ksel, vselT, kwin, vwinT, arows, ovT)
            x = _oproj(x, oT, mod, ng, b_w_out[layer - n_a].astype(BF16))
        x = _ffn(x, mod, ng, ffn_w_in[layer, 1].astype(BF16), ffn_w_out[layer, 1].astype(BF16), 2)
    return x
```

```python
import functools

import jax
import jax.numpy as jnp
from jax import lax
from jax.experimental import pallas as pl
from jax.experimental.pallas import tpu as pltpu

F32 = jnp.float32
BF16 = jnp.bfloat16

N_HEADS = 16
N_KV_GROUPS = 4
HEADS_PER_GROUP = N_HEADS // N_KV_GROUPS
HEAD_DIM = 64
N_BRANCH = 3
CMP_LEN = 32
CMP_STRIDE = 16
SLC_LEN = 64
SLC_SHIFT = 6
SLC_TOP_N = 16
WINDOW = 512
CONV_WIDTH = 3
RMS_EPS = 1e-6
NEG_INF = -1e30
FORCE_SCORE = 1e30
DEPTH = 4

TQ = 256
KC = 256
SEL_ROWS = 64
ALIBI_ROWS = 16
QSEL_ROWS = 256
QWIN_ROWS = 128
V_ROWS = HEAD_DIM + 16
MASK_BIG = float(2 ** 30)
LOG2E = 1.4426950408889634

MXU_COLS = 256
VMEM_LIMIT = 56 * 1024 * 1024
TOKEN_TILE = 512
FFN_TOKEN_TILE = 1024
FFN_ROW_SPLITS = 2


def _params(sem):
    return pltpu.CompilerParams(dimension_semantics=sem, vmem_limit_bytes=VMEM_LIMIT)


def _resident(shape):
    nd = len(shape)
    return pl.BlockSpec(shape, lambda *_: (0,) * nd, pipeline_mode=pl.Buffered(1))


def _rms(x, g):
    ms = jnp.mean(x * x, axis=-1, keepdims=True)
    return x * lax.rsqrt(ms + RMS_EPS) * g


def _pre(x, mod_ref, ng_ref, sub):
    shift = mod_ref[0, 3 * sub:3 * sub + 1, :]
    scale = mod_ref[0, 3 * sub + 1:3 * sub + 2, :]
    return _rms(x, ng_ref[2 * sub:2 * sub + 1, :]) * (1.0 + scale) + shift


def _post(x, y, mod_ref, ng_ref, sub, weight):
    gate = mod_ref[0, 3 * sub + 2:3 * sub + 3, :]
    return x + weight * gate * _rms(y, ng_ref[2 * sub + 1:2 * sub + 2, :])


def _dot(a, b):
    return jnp.dot(a, b, preferred_element_type=F32)


def _dot_nt(a, b):
    return lax.dot_general(a, b, (((1,), (1,)), ((), ())), preferred_element_type=F32)


def _dot_tn(a, b):
    return lax.dot_general(a, b, (((0,), (0,)), ((), ())), preferred_element_type=F32)


def _ada_kernel(c_ref, w_ref, b_ref, o_ref):
    c = c_ref[...]
    sc = c * jax.nn.sigmoid(c)
    o_ref[0] = _dot(sc, w_ref[0]) + b_ref[0]


def _ada(c, w, b, tn):
    L, D, N = w.shape
    B = c.shape[0]
    return pl.pallas_call(
        _ada_kernel,
        grid=(L, N // tn),
        in_specs=[
            pl.BlockSpec((B, D), lambda l, j: (0, 0)),
            pl.BlockSpec((1, D, tn), lambda l, j: (l, 0, j)),
            pl.BlockSpec((1, 1, tn), lambda l, j: (l, 0, j)),
        ],
        out_specs=pl.BlockSpec((1, B, tn), lambda l, j: (l, 0, j)),
        out_shape=jax.ShapeDtypeStruct((L, B, N), F32),
        compiler_params=_params(("parallel", "parallel")),
        name="ada",
    )(c, w, b.reshape(L, 1, N))


def _ffn_kernel(x_ref, mod_ref, ng_ref, win_ref, wout_ref, o_ref, *, sub, d_ff, n_chunks):
    tf = d_ff // n_chunks
    rows = x_ref.shape[1] // FFN_ROW_SPLITS
    for i in range(FFN_ROW_SPLITS):
        x = x_ref[0, i * rows:(i + 1) * rows, :]
        h = _pre(x, mod_ref, ng_ref, sub).astype(BF16)
        y = None
        for c in range(n_chunks):
            g = _dot(h, win_ref[:, c * tf:(c + 1) * tf])
            u = _dot(h, win_ref[:, d_ff + c * tf:d_ff + (c + 1) * tf])
            a = (g * jax.nn.sigmoid(g) * u).astype(BF16)
            part = _dot(a, wout_ref[c * tf:(c + 1) * tf, :])
            y = part if y is None else y + part
        o_ref[0, i * rows:(i + 1) * rows, :] = _post(x, y, mod_ref, ng_ref, sub, 0.5)


def _ffn(x, mod, ng, w_in, w_out, sub):
    B, S, D = x.shape
    d_ff = w_out.shape[0]
    tm = min(FFN_TOKEN_TILE, S)
    n_chunks = next(n for n in (1, 2, 4) if (d_ff // n) % MXU_COLS == 0 and d_ff % n == 0)
    return pl.pallas_call(
        functools.partial(_ffn_kernel, sub=sub, d_ff=d_ff, n_chunks=n_chunks),
        grid=(B, S // tm),
        in_specs=[
            pl.BlockSpec((1, tm, D), lambda b, i: (b, i, 0)),
            pl.BlockSpec((1, 9, D), lambda b, i: (b, 0, 0)),
            pl.BlockSpec((6, D), lambda b, i: (0, 0)),
            _resident(w_in.shape),
            _resident(w_out.shape),
        ],
        out_specs=pl.BlockSpec((1, tm, D), lambda b, i: (b, i, 0)),
        out_shape=jax.ShapeDtypeStruct(x.shape, F32),
        compiler_params=_params(("parallel", "parallel")),
        name="ffn",
    )(x, mod, ng, w_in, w_out)


def _conv_kernel(x_ref, mod_ref, ng_ref, win_ref, cw_ref, wout_ref, o_ref, vbuf, *, tm):
    D = x_ref.shape[-1]
    x = x_ref[0]
    h = _pre(x, mod_ref, ng_ref, 1).astype(BF16)
    proj = _dot(h, win_ref[...])
    b_gate = proj[:, :D]
    v = proj[:, D:2 * D] * proj[:, 2 * D:]

    @pl.when(pl.program_id(1) == 0)
    def _():
        vbuf[0:8, :] = jnp.zeros((8, D), F32)

    vbuf[8:8 + tm, :] = v
    y = (cw_ref[2:3, :] * v
         + cw_ref[1:2, :] * vbuf[7:7 + tm, :]
         + cw_ref[0:1, :] * vbuf[6:6 + tm, :])
    vbuf[0:8, :] = v[tm - 8:, :]
    z = (b_gate * y).astype(BF16)
    o_ref[0] = _post(x, _dot(z, wout_ref[...]), mod_ref, ng_ref, 1, 1.0)


def _conv_mixer(x, mod, ng, w_in, conv_w, w_out):
    B, S, D = x.shape
    tm = min(TOKEN_TILE, S)
    return pl.pallas_call(
        functools.partial(_conv_kernel, tm=tm),
        grid=(B, S // tm),
        in_specs=[
            pl.BlockSpec((1, tm, D), lambda b, i: (b, i, 0)),
            pl.BlockSpec((1, 9, D), lambda b, i: (b, 0, 0)),
            pl.BlockSpec((6, D), lambda b, i: (0, 0)),
            _resident(w_in.shape),
            pl.BlockSpec(conv_w.shape, lambda b, i: (0, 0)),
            _resident(w_out.shape),
        ],
        out_specs=pl.BlockSpec((1, tm, D), lambda b, i: (b, i, 0)),
        out_shape=jax.ShapeDtypeStruct(x.shape, F32),
        scratch_shapes=[pltpu.VMEM((tm + 8, D), F32)],
        compiler_params=_params(("parallel", "arbitrary")),
        name="conv_mixer",
    )(x, mod, ng, w_in, conv_w, w_out)


def _pos_cols(shape, first, hi, lo):
    lane = lax.broadcasted_iota(jnp.int32, shape, 1)
    hi_b = jnp.broadcast_to(hi, shape)
    lo_b = jnp.broadcast_to(lo, shape)
    return jnp.where((lane >= first) & (lane < first + 3), hi_b,
                     jnp.where((lane >= first + 3) & (lane < first + 6), lo_b, 0.0))


def _kv_kernel(x_ref, mod_ref, g_ref, w0_ref, wks_ref, wkw_ref, wvsT_ref, wvwT_ref,
               kv0_ref, ksel_ref, kwin_ref, vselT_ref, vwinT_ref, *, tm):
    G = N_KV_GROUPS
    x = x_ref[0]
    h = (_rms(x, g_ref[...]) * (1.0 + mod_ref[0, 1:2, :]) + mod_ref[0, 0:1, :]).astype(BF16)
    kv0_ref[0] = _dot(h, w0_ref[...])

    tok = pl.program_id(1) * tm + lax.broadcasted_iota(jnp.int32, (tm, 1), 0)
    blk = tok >> SLC_SHIFT
    blk64 = (blk * SLC_LEN).astype(F32)
    off = (tok & (SLC_LEN - 1)).astype(F32)
    lane = lax.broadcasted_iota(jnp.int32, (tm, QSEL_ROWS), 1)
    onehot = ((lane - HEAD_DIM) == blk).astype(F32)
    sel_const = onehot + _pos_cols((tm, QSEL_ROWS), HEAD_DIM + SEL_ROWS, blk64, off)
    win_const = _pos_cols((tm, QWIN_ROWS), HEAD_DIM, blk64, off)

    ks = _dot(h, wks_ref[...])
    kw = _dot(h, wkw_ref[...])
    for g in range(G):
        ksel_ref[0, g] = (ks[:, g * QSEL_ROWS:(g + 1) * QSEL_ROWS] + sel_const).astype(BF16)
        kwin_ref[0, g] = (kw[:, g * QWIN_ROWS:(g + 1) * QWIN_ROWS] + win_const).astype(BF16)

    ones_rows = (lax.broadcasted_iota(jnp.int32, (V_ROWS - HEAD_DIM, KC), 0) == 0).astype(F32)
    for wT_ref, out_ref in ((wvsT_ref, vselT_ref), (wvwT_ref, vwinT_ref)):
        vT = _dot_nt(wT_ref[...], h)
        for g in range(G):
            for c in range(tm // KC):
                blk_v = vT[g * HEAD_DIM:(g + 1) * HEAD_DIM, c * KC:(c + 1) * KC]
                out_ref[0, g, c] = jnp.concatenate([blk_v, ones_rows], axis=0).astype(BF16)


def _build_kv(x, kvmod, kv_norm_g, w0, wks, wkw, wvsT, wvwT):
    B, S, D = x.shape
    G = N_KV_GROUPS
    tm = min(TOKEN_TILE, S)
    out_shape = (
        jax.ShapeDtypeStruct((B, S, w0.shape[1]), F32),
        jax.ShapeDtypeStruct((B, G, S, QSEL_ROWS), BF16),
        jax.ShapeDtypeStruct((B, G, S, QWIN_ROWS), BF16),
        jax.ShapeDtypeStruct((B, G, S // KC, V_ROWS, KC), BF16),
        jax.ShapeDtypeStruct((B, G, S // KC, V_ROWS, KC), BF16),
    )
    return pl.pallas_call(
        functools.partial(_kv_kernel, tm=tm),
        grid=(B, S // tm),
        in_specs=[
            pl.BlockSpec((1, tm, D), lambda b, i: (b, i, 0)),
            pl.BlockSpec((1, 2, D), lambda b, i: (b, 0, 0)),
            pl.BlockSpec((1, D), lambda b, i: (0, 0)),
            _resident(w0.shape), _resident(wks.shape), _resident(wkw.shape),
            _resident(wvsT.shape), _resident(wvwT.shape),
        ],
        out_specs=(
            pl.BlockSpec((1, tm, w0.shape[1]), lambda b, i: (b, i, 0)),
            pl.BlockSpec((1, G, tm, QSEL_ROWS), lambda b, i: (b, 0, i, 0)),
            pl.BlockSpec((1, G, tm, QWIN_ROWS), lambda b, i: (b, 0, i, 0)),
            pl.BlockSpec((1, G, tm // KC, V_ROWS, KC), lambda b, i: (b, 0, i, 0, 0)),
            pl.BlockSpec((1, G, tm // KC, V_ROWS, KC), lambda b, i: (b, 0, i, 0, 0)),
        ),
        out_shape=out_shape,
        compiler_params=_params(("parallel", "parallel")),
        name="build_kv",
    )(x, kvmod, kv_norm_g.reshape(1, D), w0, wks, wkw, wvsT, wvwT)


def _gelu_tanh(x):
    return 0.5 * x * (1.0 + jnp.tanh(0.7978845608028654 * (x + 0.044715 * (x * x * x))))


def _cmp_kernel(ck_ref, cv_ref, pos_ref, w1_ref, w2k_ref, w2vT_ref, kc_ref, vcT_ref, *, nc):
    def hidden(c, kv):
        a = _dot((c + pos_ref[2 * kv:2 * kv + 1, :]).astype(BF16), w1_ref[kv, 0])
        b = _dot((c + pos_ref[2 * kv + 1:2 * kv + 2, :]).astype(BF16), w1_ref[kv, 1])
        return _gelu_tanh(a + pltpu.roll(b, nc - 1, 0))

    hk = hidden(ck_ref[0, 0], 0).astype(BF16)
    hv = hidden(cv_ref[0, 0], 1).astype(BF16)
    n = lax.broadcasted_iota(jnp.int32, (nc, 1), 0)
    end_hi = (CMP_STRIDE * (n + 1)).astype(F32)
    end_lo = jnp.full((nc, 1), float(CMP_LEN - 1 - CMP_STRIDE), F32)
    kc = _dot(hk, w2k_ref[...]) + _pos_cols((nc, QWIN_ROWS), HEAD_DIM, end_hi, end_lo)
    kc_ref[0, 0] = kc.astype(BF16)
    vcT_ref[0, 0] = _dot_nt(w2vT_ref[...], hv).astype(BF16)


def _compress(ck, cv, pos, w1, w2k, w2vT):
    B, G, nc, F = ck.shape
    return pl.pallas_call(
        functools.partial(_cmp_kernel, nc=nc),
        grid=(B, G),
        in_specs=[
            pl.BlockSpec((1, 1, nc, F), lambda b, g: (b, g, 0, 0)),
            pl.BlockSpec((1, 1, nc, F), lambda b, g: (b, g, 0, 0)),
            pl.BlockSpec(pos.shape, lambda b, g: (0, 0)),
            _resident(w1.shape), _resident(w2k.shape), _resident(w2vT.shape),
        ],
        out_specs=(
            pl.BlockSpec((1, 1, nc, QWIN_ROWS), lambda b, g: (b, g, 0, 0)),
            pl.BlockSpec((1, 1, HEAD_DIM, nc), lambda b, g: (b, g, 0, 0)),
        ),
        out_shape=(
            jax.ShapeDtypeStruct((B, G, nc, QWIN_ROWS), BF16),
            jax.ShapeDtypeStruct((B, G, HEAD_DIM, nc), BF16),
        ),
        compiler_params=_params(("parallel", "parallel")),
        name="compress_kv",
    )(ck, cv, pos, w1, w2k, w2vT)


def _qproj_kernel(x_ref, mod_ref, ng_ref, wqT_ref, wgT_ref, qT_ref, gT_ref):
    h = _pre(x_ref[0], mod_ref, ng_ref, 1).astype(BF16)
    qT_ref[0] = (_dot_nt(wqT_ref[...], h) * (HEAD_DIM ** -0.5 * LOG2E)).astype(BF16)
    gT_ref[0] = jax.nn.sigmoid(_dot_nt(wgT_ref[...], h))


def _qproj(x, mod, ng, wqT, wgT):
    B, S, D = x.shape
    tm = min(TOKEN_TILE, S)
    return pl.pallas_call(
        _qproj_kernel,
        grid=(B, S // tm),
        in_specs=[
            pl.BlockSpec((1, tm, D), lambda b, i: (b, i, 0)),
            pl.BlockSpec((1, 9, D), lambda b, i: (b, 0, 0)),
            pl.BlockSpec((6, D), lambda b, i: (0, 0)),
            _resident(wqT.shape), _resident(wgT.shape),
        ],
        out_specs=(
            pl.BlockSpec((1, wqT.shape[0], tm), lambda b, i: (b, 0, i)),
            pl.BlockSpec((1, wgT.shape[0], tm), lambda b, i: (b, 0, i)),
        ),
        out_shape=(
            jax.ShapeDtypeStruct((B, wqT.shape[0], S), BF16),
            jax.ShapeDtypeStruct((B, wgT.shape[0], S), F32),
        ),
        compiler_params=_params(("parallel", "parallel")),
        name="qproj",
    )(x, mod, ng, wqT, wgT)


def _attn_kernel(qT_ref, gT_ref, kc_ref, vcT_ref, ksel_ref, vselT_ref, kwin_ref, vwinT_ref,
                 arows_ref, ovT_ref, oT_ref, qsel_ref, qwin_ref, m_ref, acc_ref, ocmp_ref, owin_ref,
                 s_buf, p_buf, a_buf, chunks_ref, *, ns, nc, top_n):
    R = HEADS_PER_GROUP
    dh = HEAD_DIM
    qi = pl.program_id(2)
    q0 = qi * TQ
    t_row = q0 + lax.broadcasted_iota(jnp.int32, (1, TQ), 1)

    def lanes(r):
        return slice(r * TQ, (r + 1) * TQ)

    pad_win = jnp.zeros((QWIN_ROWS - dh - ALIBI_ROWS, TQ), BF16)
    pad_sel = jnp.zeros((QSEL_ROWS - dh - SEL_ROWS - ALIBI_ROWS, TQ), BF16)
    for r in range(R):
        qwin_ref[:, lanes(r)] = jnp.concatenate(
            [qT_ref[0, r * dh:(r + 1) * dh, :], arows_ref[0, r], pad_win], axis=0)

    n_io = lax.broadcasted_iota(jnp.int32, (nc, R * TQ), 0)
    t_all = q0 + (lax.broadcasted_iota(jnp.int32, (nc, R * TQ), 1) & (TQ - 1))
    valid_c = (CMP_STRIDE * n_io + (CMP_LEN - 1)) <= t_all
    s = jnp.where(valid_c, _dot(kc_ref[0, 0], qwin_ref[...]), NEG_INF)
    e = jnp.where(valid_c, jnp.exp2(s - jnp.max(s, axis=0, keepdims=True)), 0.0)
    p = e / jnp.maximum(jnp.sum(e, axis=0, keepdims=True), 1e-30)
    o_cmp = _dot(vcT_ref[0, 0], p.astype(BF16))
    psum = p[:, lanes(0)]
    for r in range(1, R):
        psum = psum + p[:, lanes(r)]

    p_hi = psum.astype(BF16)
    rem = psum - p_hi.astype(F32)
    p_mid = rem.astype(BF16)
    p_lo = (rem - p_mid.astype(F32)).astype(BF16)
    ovT = ovT_ref[...]
    imp = _dot(ovT, p_hi) + _dot(ovT, p_mid) + _dot(ovT, p_lo)
    j_io = lax.broadcasted_iota(jnp.int32, (ns, TQ), 0)
    cur = t_row >> SLC_SHIFT
    forced = (j_io == 0) | (j_io == cur) | (j_io == cur - 1)
    future = (j_io * SLC_LEN) > t_row
    imp = jnp.where(forced, FORCE_SCORE, jnp.where(future, NEG_INF, imp))

    ocmp_ref[...] = o_cmp

    k_io = lax.broadcasted_iota(jnp.int32, (KC, R * TQ), 0)
    q_io = lax.broadcasted_iota(jnp.int32, (KC, R * TQ), 1) & (TQ - 1)
    causal = k_io <= q_io
    band = k_io > q_io

    def win_scores(c):
        start = pl.multiple_of(c * KC, KC)
        return _dot(kwin_ref[0, 0, pl.ds(start, KC), :], qwin_ref[...])

    c_mid = jnp.maximum(qi - 1, 0)
    c_far = jnp.maximum(qi - 2, 0)
    s_near = jnp.where(causal, win_scores(qi), NEG_INF)
    s_mid = jnp.where(qi >= 1, win_scores(c_mid), NEG_INF)
    s_far = jnp.where(band & (qi >= 2), win_scores(c_far), NEG_INF)
    m_win = jnp.maximum(jnp.max(s_near, axis=0, keepdims=True),
                        jnp.maximum(jnp.max(s_mid, axis=0, keepdims=True),
                                    jnp.max(s_far, axis=0, keepdims=True)))
    acc_win = (_dot(vwinT_ref[0, 0, qi], jnp.exp2(s_near - m_win).astype(BF16))
               + _dot(vwinT_ref[0, 0, c_mid], jnp.exp2(s_mid - m_win).astype(BF16))
               + _dot(vwinT_ref[0, 0, c_far], jnp.exp2(s_far - m_win).astype(BF16)))
    owin_ref[...] = acc_win[:dh, :] / acc_win[dh:dh + 1, :]

    groups = [imp[8 * k:8 * k + 8, :] for k in range(ns // 8)]
    ranks = [jnp.zeros((8, TQ), jnp.int32) for _ in groups]
    sub_io = lax.broadcasted_iota(jnp.int32, (8, TQ), 0)
    for jp in range(ns):
        row = imp[jp:jp + 1, :]
        for k, grp in enumerate(groups):
            if 8 * k > jp:
                beats = (row >= grp).astype(jnp.int32)
            elif 8 * k + 7 < jp:
                beats = (row > grp).astype(jnp.int32)
            else:
                beats = jnp.where(sub_io > jp - 8 * k, (row >= grp).astype(jnp.int32),
                                  (row > grp).astype(jnp.int32))
            ranks[k] = ranks[k] + beats
    rank = jnp.concatenate(ranks, axis=0)
    sel_rows = jnp.where(rank < top_n, 0.0, -MASK_BIG).astype(BF16)

    row_used = jnp.max(jnp.where(rank < top_n, 1.0, 0.0), axis=1, keepdims=True)
    blocks_per_chunk = KC // SLC_LEN
    n_full = jnp.int32(0)
    for c in range(ns // blocks_per_chunk):
        used = jnp.max(row_used[c * blocks_per_chunk:(c + 1) * blocks_per_chunk, :]) > 0.0
        chunks_ref[n_full] = c
        n_full = n_full + jnp.where(used & (c < qi), 1, 0)
    chunks_ref[n_full] = qi
    if ns < SEL_ROWS:
        sel_rows = jnp.concatenate([sel_rows, jnp.zeros((SEL_ROWS - ns, TQ), BF16)], axis=0)
    for r in range(R):
        qsel_ref[:, lanes(r)] = jnp.concatenate(
            [qT_ref[0, r * dh:(r + 1) * dh, :], sel_rows, arows_ref[0, r], pad_sel], axis=0)

    def sel_scores(c):
        start = pl.multiple_of(c * KC, KC)
        return _dot(ksel_ref[0, 0, pl.ds(start, KC), :], qsel_ref[...])

    def sel_softmax(slot, mask):
        sc = s_buf[slot]
        if mask is not None:
            sc = jnp.where(mask, sc, NEG_INF)
        m_old = m_ref[...]
        m_new = jnp.maximum(m_old, jnp.max(sc, axis=0, keepdims=True))
        p_buf[slot] = jnp.exp2(sc - m_new).astype(BF16)
        a_buf[slot] = jnp.exp2(m_old - m_new)
        m_ref[...] = m_new

    def sel_values(c, slot):
        acc_ref[...] = a_buf[slot] * acc_ref[...] + _dot(vselT_ref[0, 0, c], p_buf[slot])

    m_ref[...] = jnp.full(m_ref.shape, NEG_INF, F32)
    acc_ref[...] = jnp.zeros(acc_ref.shape, F32)
    p_buf[1] = jnp.zeros(p_buf.shape[1:], BF16)
    a_buf[1] = jnp.ones(a_buf.shape[1:], F32)
    def chunk_at(k):
        return chunks_ref[jnp.maximum(k, 0)]

    s_buf[0] = sel_scores(chunk_at(0))

    def sel_pair(i, carry):
        k = 2 * i
        s_buf[1] = sel_scores(chunk_at(k + 1))
        sel_softmax(0, None)
        sel_values(chunk_at(k - 1), 1)
        s_buf[0] = sel_scores(chunk_at(k + 2))
        sel_softmax(1, None)
        sel_values(chunk_at(k), 0)
        return carry

    lax.fori_loop(0, n_full >> 1, sel_pair, 0)

    @pl.when((n_full & 1) == 1)
    def _():
        sel_softmax(0, None)
        s_buf[1] = sel_scores(qi)
        sel_values(chunk_at(n_full - 2), 1)
        sel_softmax(1, causal)
        sel_values(chunk_at(n_full - 1), 0)
        sel_values(qi, 1)

    @pl.when((n_full & 1) == 0)
    def _():
        sel_softmax(0, causal)
        sel_values(chunk_at(n_full - 1), 1)
        sel_values(qi, 0)

    acc = acc_ref[...]
    o_sel = acc[:dh, :] / acc[dh:dh + 1, :]
    for r in range(R):
        o = (gT_ref[0, r:r + 1, :] * ocmp_ref[:, lanes(r)]
             + gT_ref[0, R + r:R + r + 1, :] * o_sel[:, lanes(r)]
             + gT_ref[0, 2 * R + r:2 * R + r + 1, :] * owin_ref[:, lanes(r)])
        oT_ref[0, r * dh:(r + 1) * dh, :] = o.astype(BF16)


def _attention(qT, gT, kc, vcT, ksel, vselT, kwin, vwinT, arows, ovT):
    B, HD, S = qT.shape
    G, R, dh = N_KV_GROUPS, HEADS_PER_GROUP, HEAD_DIM
    nc = kc.shape[2]
    ns = S // SLC_LEN
    n_kc = S // KC
    return pl.pallas_call(
        functools.partial(_attn_kernel, ns=ns, nc=nc, top_n=min(SLC_TOP_N, ns)),
        grid=(B, G, S // TQ),
        in_specs=[
            pl.BlockSpec((1, R * dh, TQ), lambda b, g, i: (b, g, i)),
            pl.BlockSpec((1, gT.shape[1] // G, TQ), lambda b, g, i: (b, g, i)),
            pl.BlockSpec((1, 1, nc, QWIN_ROWS), lambda b, g, i: (b, g, 0, 0)),
            pl.BlockSpec((1, 1, dh, nc), lambda b, g, i: (b, g, 0, 0)),
            pl.BlockSpec((1, 1, S, QSEL_ROWS), lambda b, g, i: (b, g, 0, 0)),
            pl.BlockSpec((1, 1, n_kc, V_ROWS, KC), lambda b, g, i: (b, g, 0, 0, 0)),
            pl.BlockSpec((1, 1, S, QWIN_ROWS), lambda b, g, i: (b, g, 0, 0)),
            pl.BlockSpec((1, 1, n_kc, V_ROWS, KC), lambda b, g, i: (b, g, 0, 0, 0)),
            pl.BlockSpec((1, R, ALIBI_ROWS, TQ), lambda b, g, i: (g, 0, 0, 0)),
            pl.BlockSpec(ovT.shape, lambda b, g, i: (0, 0)),
        ],
        out_specs=pl.BlockSpec((1, R * dh, TQ), lambda b, g, i: (b, g, i)),
        out_shape=jax.ShapeDtypeStruct((B, HD, S), BF16),
        scratch_shapes=[
            pltpu.VMEM((QSEL_ROWS, R * TQ), BF16),
            pltpu.VMEM((QWIN_ROWS, R * TQ), BF16),
            pltpu.VMEM((1, R * TQ), F32),
            pltpu.VMEM((V_ROWS, R * TQ), F32),
            pltpu.VMEM((dh, R * TQ), F32),
            pltpu.VMEM((dh, R * TQ), F32),
            pltpu.VMEM((2, KC, R * TQ), F32),
            pltpu.VMEM((2, KC, R * TQ), BF16),
            pltpu.VMEM((2, 1, R * TQ), F32),
            pltpu.SMEM((n_kc + 1,), jnp.int32),
        ],
        compiler_params=_params(("parallel", "parallel", "arbitrary")),
        name="nsa_attention",
    )(qT, gT, kc, vcT, ksel, vselT, kwin, vwinT, arows, ovT)


def _oproj_kernel(x_ref, oT_ref, mod_ref, ng_ref, w_ref, o_ref):
    y = _dot_tn(oT_ref[0], w_ref[...])
    o_ref[0] = _post(x_ref[0], y, mod_ref, ng_ref, 1, 1.0)


def _oproj(x, oT, mod, ng, w_out):
    B, S, D = x.shape
    tm = min(TOKEN_TILE, S)
    return pl.pallas_call(
        _oproj_kernel,
        grid=(B, S // tm),
        in_specs=[
            pl.BlockSpec((1, tm, D), lambda b, i: (b, i, 0)),
            pl.BlockSpec((1, oT.shape[1], tm), lambda b, i: (b, 0, i)),
            pl.BlockSpec((1, 9, D), lambda b, i: (b, 0, 0)),
            pl.BlockSpec((6, D), lambda b, i: (0, 0)),
            _resident(w_out.shape),
        ],
        out_specs=pl.BlockSpec((1, tm, D), lambda b, i: (b, i, 0)),
        out_shape=jax.ShapeDtypeStruct(x.shape, F32),
        compiler_params=_params(("parallel", "parallel")),
        name="oproj",
    )(x, oT, mod, ng, w_out)


def _split3(v):
    hi = v.astype(BF16)
    r1 = v - hi.astype(F32)
    mid = r1.astype(BF16)
    lo = (r1 - mid.astype(F32)).astype(BF16)
    return hi, mid, lo


def _alibi_rows():
    h = jnp.arange(1, N_HEADS + 1, dtype=F32)
    slopes = jnp.exp2(-8.0 * h / N_HEADS).reshape(N_KV_GROUPS, HEADS_PER_GROUP)
    parts = _split3(slopes * LOG2E)
    rows = jnp.stack(list(parts) * 2 + [jnp.zeros_like(parts[0])] * (ALIBI_ROWS - 6), axis=-1)
    return jnp.broadcast_to(rows[..., None], rows.shape + (TQ,))


def _overlap_t(ns, nc, n_cmp):
    i = jnp.arange(nc)[None, :]
    j = jnp.arange(ns)[:, None]
    start = i * CMP_STRIDE
    ov = (start < (j + 1) * SLC_LEN) & (start + CMP_LEN - 1 >= j * SLC_LEN) & (i < n_cmp)
    return ov.astype(BF16)


def _pad_cols(w, width):
    return jnp.pad(w, ((0, 0), (0, width - w.shape[1])))


def kernel(x, c, ada_w, ada_b, norm_g, ffn_w_in, ffn_w_out, a_w_in, a_conv, a_w_out,
           kv_norm_g, kv_ada_w, kv_ada_b, kv_w, cmp_pos, cmp_w1, cmp_w2, b_w_in, b_w_out):
    B, S, D = x.shape
    G, R, dh = N_KV_GROUPS, HEADS_PER_GROUP, HEAD_DIM
    n_a = DEPTH // 2
    assert S % TQ == 0 and WINDOW == 2 * KC and TQ == KC
    assert D == N_HEADS * dh

    mods = _ada(c, ada_w, ada_b, tn=2304).reshape(DEPTH, B, 9, D)
    kvmod = _ada(c, kv_ada_w[None], kv_ada_b[None], tn=1024).reshape(B, 2, D)

    arows = _alibi_rows()
    nc = S // CMP_STRIDE
    n_cmp = (S - CMP_LEN) // CMP_STRIDE + 1
    ovT = _overlap_t(S // SLC_LEN, nc, n_cmp)

    kv_state = None
    for layer in range(DEPTH):
        mod = mods[layer]
        ng = norm_g[layer].reshape(6, D)
        if layer == n_a:
            gcols = G * dh
            w0 = kv_w[:, :2 * gcols].astype(BF16)
            k_sel = kv_w[:, 2 * gcols:3 * gcols]
            k_win = kv_w[:, 4 * gcols:5 * gcols]
            wks = jnp.concatenate(
                [_pad_cols(k_sel[:, g * dh:(g + 1) * dh], QSEL_ROWS) for g in range(G)], axis=1)
            wkw = jnp.concatenate(
                [_pad_cols(k_win[:, g * dh:(g + 1) * dh], QWIN_ROWS) for g in range(G)], axis=1)
            wvsT = kv_w[:, 3 * gcols:4 * gcols].T.astype(BF16)
            wvwT = kv_w[:, 5 * gcols:6 * gcols].T.astype(BF16)
            kv0, ksel, kwin, vselT, vwinT = _build_kv(
                x, kvmod, kv_norm_g, w0, wks.astype(BF16), wkw.astype(BF16), wvsT, wvwT)

            def chunked(a):
                a = a.reshape(B, S, G, dh).transpose(0, 2, 1, 3)
                return a.reshape(B, G, nc, CMP_STRIDE * dh)

            half = CMP_STRIDE * dh
            kc, vcT = _compress(
                chunked(kv0[..., :gcols]), chunked(kv0[..., gcols:]),
                cmp_pos.reshape(4, half),
                cmp_w1.reshape(2, 2, half, cmp_w1.shape[-1]).astype(BF16),
                _pad_cols(cmp_w2[0], QWIN_ROWS).astype(BF16),
                cmp_w2[1].T.astype(BF16))
            kv_state = (kc, vcT, ksel, vselT, kwin, vwinT)

        x = _ffn(x, mod, ng, ffn_w_in[layer, 0].astype(BF16), ffn_w_out[layer, 0].astype(BF16), 0)
        if layer < n_a:
            x = _conv_mixer(x, mod, ng, a_w_in[layer].astype(BF16), a_conv[layer],
                            a_w_out[layer].astype(BF16))
        else:
            w_in = b_w_in[layer - n_a]
            wqT = w_in[:, :N_HEADS * dh].T.astype(BF16)
            wg = w_in[:, N_HEADS * dh:].reshape(D, N_BRANCH, G, R)
            wg = jnp.pad(wg.transpose(2, 1, 3, 0), ((0, 0), (0, 1), (0, 0), (0, 0)))
            wgT = wg.reshape(G * (N_BRANCH + 1) * R, D).astype(BF16)
            qT, gT = _qproj(x, mod, ng, wqT, wgT)
            kc, vcT, ksel, vselT, kwin, vwinT = kv_state
            oT = _attention(qT, gT, kc, vcT, ksel, vselT, kwin, vwinT, arows, ovT)
            x = _oproj(x, oT, mod, ng, b_w_out[layer - n_a].astype(BF16))
        x = _ffn(x, mod, ng, ffn_w_in[layer, 1].astype(BF16), ffn_w_out[layer, 1].astype(BF16), 2)
    return x
```

```python
import functools

import jax
import jax.numpy as jnp
from jax import lax
from jax.experimental import pallas as pl
from jax.experimental.pallas import tpu as pltpu

F32 = jnp.float32
BF16 = jnp.bfloat16

N_HEADS = 16
N_KV_GROUPS = 4
HEADS_PER_GROUP = N_HEADS // N_KV_GROUPS
HEAD_DIM = 64
N_BRANCH = 3
CMP_LEN = 32
CMP_STRIDE = 16
SLC_LEN = 64
SLC_SHIFT = 6
SLC_TOP_N = 16
WINDOW = 512
CONV_WIDTH = 3
RMS_EPS = 1e-6
NEG_INF = -1e30
FORCE_SCORE = 1e30
DEPTH = 4

TQ = 256
KC = 256
SEL_ROWS = 64
ALIBI_ROWS = 16
QSEL_ROWS = 256
QWIN_ROWS = 128
V_ROWS = HEAD_DIM + 16
MASK_BIG = float(2 ** 30)
LOG2E = 1.4426950408889634

MXU_COLS = 256
VMEM_LIMIT = 56 * 1024 * 1024
TOKEN_TILE = 512
FFN_TOKEN_TILE = 1024
FFN_ROW_SPLITS = 4


def _params(sem):
    return pltpu.CompilerParams(dimension_semantics=sem, vmem_limit_bytes=VMEM_LIMIT)


def _resident(shape):
    nd = len(shape)
    return pl.BlockSpec(shape, lambda *_: (0,) * nd, pipeline_mode=pl.Buffered(1))


def _rms(x, g):
    ms = jnp.mean(x * x, axis=-1, keepdims=True)
    return x * lax.rsqrt(ms + RMS_EPS) * g


def _pre(x, mod_ref, ng_ref, sub):
    shift = mod_ref[0, 3 * sub:3 * sub + 1, :]
    scale = mod_ref[0, 3 * sub + 1:3 * sub + 2, :]
    return _rms(x, ng_ref[2 * sub:2 * sub + 1, :]) * (1.0 + scale) + shift


def _post(x, y, mod_ref, ng_ref, sub, weight):
    gate = mod_ref[0, 3 * sub + 2:3 * sub + 3, :]
    return x + weight * gate * _rms(y, ng_ref[2 * sub + 1:2 * sub + 2, :])


def _dot(a, b):
    return jnp.dot(a, b, preferred_element_type=F32)


def _dot_nt(a, b):
    return lax.dot_general(a, b, (((1,), (1,)), ((), ())), preferred_element_type=F32)


def _dot_tn(a, b):
    return lax.dot_general(a, b, (((0,), (0,)), ((), ())), preferred_element_type=F32)


def _ada_kernel(c_ref, w_ref, b_ref, o_ref):
    c = c_ref[...]
    sc = c * jax.nn.sigmoid(c)
    o_ref[0] = _dot(sc, w_ref[0]) + b_ref[0]


def _ada(c, w, b, tn):
    L, D, N = w.shape
    B = c.shape[0]
    return pl.pallas_call(
        _ada_kernel,
        grid=(L, N // tn),
        in_specs=[
            pl.BlockSpec((B, D), lambda l, j: (0, 0)),
            pl.BlockSpec((1, D, tn), lambda l, j: (l, 0, j)),
            pl.BlockSpec((1, 1, tn), lambda l, j: (l, 0, j)),
        ],
        out_specs=pl.BlockSpec((1, B, tn), lambda l, j: (l, 0, j)),
        out_shape=jax.ShapeDtypeStruct((L, B, N), F32),
        compiler_params=_params(("parallel", "parallel")),
        name="ada",
    )(c, w, b.reshape(L, 1, N))


def _ffn_kernel(x_ref, mod_ref, ng_ref, win_ref, wout_ref, o_ref, *, sub, d_ff, n_chunks):
    tf = d_ff // n_chunks
    rows = x_ref.shape[1] // FFN_ROW_SPLITS
    for i in range(FFN_ROW_SPLITS):
        x = x_ref[0, i * rows:(i + 1) * rows, :]
        h = _pre(x, mod_ref, ng_ref, sub).astype(BF16)
        y = None
        for c in range(n_chunks):
            g = _dot(h, win_ref[:, c * tf:(c + 1) * tf])
            u = _dot(h, win_ref[:, d_ff + c * tf:d_ff + (c + 1) * tf])
            a = (g * jax.nn.sigmoid(g) * u).astype(BF16)
            part = _dot(a, wout_ref[c * tf:(c + 1) * tf, :])
            y = part if y is None else y + part
        o_ref[0, i * rows:(i + 1) * rows, :] = _post(x, y, mod_ref, ng_ref, sub, 0.5)


def _ffn(x, mod, ng, w_in, w_out, sub):
    B, S, D = x.shape
    d_ff = w_out.shape[0]
    tm = min(FFN_TOKEN_TILE, S)
    n_chunks = next(n for n in (1, 2, 4) if (d_ff // n) % MXU_COLS == 0 and d_ff % n == 0)
    return pl.pallas_call(
        functools.partial(_ffn_kernel, sub=sub, d_ff=d_ff, n_chunks=n_chunks),
        grid=(B, S // tm),
        in_specs=[
            pl.BlockSpec((1, tm, D), lambda b, i: (b, i, 0)),
            pl.BlockSpec((1, 9, D), lambda b, i: (b, 0, 0)),
            pl.BlockSpec((6, D), lambda b, i: (0, 0)),
            _resident(w_in.shape),
            _resident(w_out.shape),
        ],
        out_specs=pl.BlockSpec((1, tm, D), lambda b, i: (b, i, 0)),
        out_shape=jax.ShapeDtypeStruct(x.shape, F32),
        compiler_params=_params(("parallel", "parallel")),
        name="ffn",
    )(x, mod, ng, w_in, w_out)


def _conv_kernel(x_ref, mod_ref, ng_ref, win_ref, cw_ref, wout_ref, o_ref, vbuf, *, tm):
    D = x_ref.shape[-1]
    x = x_ref[0]
    h = _pre(x, mod_ref, ng_ref, 1).astype(BF16)
    proj = _dot(h, win_ref[...])
    b_gate = proj[:, :D]
    v = proj[:, D:2 * D] * proj[:, 2 * D:]

    @pl.when(pl.program_id(1) == 0)
    def _():
        vbuf[0:8, :] = jnp.zeros((8, D), F32)

    vbuf[8:8 + tm, :] = v
    y = (cw_ref[2:3, :] * v
         + cw_ref[1:2, :] * vbuf[7:7 + tm, :]
         + cw_ref[0:1, :] * vbuf[6:6 + tm, :])
    vbuf[0:8, :] = v[tm - 8:, :]
    z = (b_gate * y).astype(BF16)
    o_ref[0] = _post(x, _dot(z, wout_ref[...]), mod_ref, ng_ref, 1, 1.0)


def _conv_mixer(x, mod, ng, w_in, conv_w, w_out):
    B, S, D = x.shape
    tm = min(TOKEN_TILE, S)
    return pl.pallas_call(
        functools.partial(_conv_kernel, tm=tm),
        grid=(B, S // tm),
        in_specs=[
            pl.BlockSpec((1, tm, D), lambda b, i: (b, i, 0)),
            pl.BlockSpec((1, 9, D), lambda b, i: (b, 0, 0)),
            pl.BlockSpec((6, D), lambda b, i: (0, 0)),
            _resident(w_in.shape),
            pl.BlockSpec(conv_w.shape, lambda b, i: (0, 0)),
            _resident(w_out.shape),
        ],
        out_specs=pl.BlockSpec((1, tm, D), lambda b, i: (b, i, 0)),
        out_shape=jax.ShapeDtypeStruct(x.shape, F32),
        scratch_shapes=[pltpu.VMEM((tm + 8, D), F32)],
        compiler_params=_params(("parallel", "arbitrary")),
        name="conv_mixer",
    )(x, mod, ng, w_in, conv_w, w_out)


def _pos_cols(shape, first, hi, lo):
    lane = lax.broadcasted_iota(jnp.int32, shape, 1)
    hi_b = jnp.broadcast_to(hi, shape)
    lo_b = jnp.broadcast_to(lo, shape)
    return jnp.where((lane >= first) & (lane < first + 3), hi_b,
                     jnp.where((lane >= first + 3) & (lane < first + 6), lo_b, 0.0))


def _kv_kernel(x_ref, mod_ref, g_ref, w0_ref, wks_ref, wkw_ref, wvsT_ref, wvwT_ref,
               kv0_ref, ksel_ref, kwin_ref, vselT_ref, vwinT_ref, *, tm):
    G = N_KV_GROUPS
    x = x_ref[0]
    h = (_rms(x, g_ref[...]) * (1.0 + mod_ref[0, 1:2, :]) + mod_ref[0, 0:1, :]).astype(BF16)
    kv0_ref[0] = _dot(h, w0_ref[...])

    tok = pl.program_id(1) * tm + lax.broadcasted_iota(jnp.int32, (tm, 1), 0)
    blk = tok >> SLC_SHIFT
    blk64 = (blk * SLC_LEN).astype(F32)
    off = (tok & (SLC_LEN - 1)).astype(F32)
    lane = lax.broadcasted_iota(jnp.int32, (tm, QSEL_ROWS), 1)
    onehot = ((lane - HEAD_DIM) == blk).astype(F32)
    sel_const = onehot + _pos_cols((tm, QSEL_ROWS), HEAD_DIM + SEL_ROWS, blk64, off)
    win_const = _pos_cols((tm, QWIN_ROWS), HEAD_DIM, blk64, off)

    ks = _dot(h, wks_ref[...])
    kw = _dot(h, wkw_ref[...])
    for g in range(G):
        ksel_ref[0, g] = (ks[:, g * QSEL_ROWS:(g + 1) * QSEL_ROWS] + sel_const).astype(BF16)
        kwin_ref[0, g] = (kw[:, g * QWIN_ROWS:(g + 1) * QWIN_ROWS] + win_const).astype(BF16)

    ones_rows = (lax.broadcasted_iota(jnp.int32, (V_ROWS - HEAD_DIM, KC), 0) == 0).astype(F32)
    for wT_ref, out_ref in ((wvsT_ref, vselT_ref), (wvwT_ref, vwinT_ref)):
        vT = _dot_nt(wT_ref[...], h)
        for g in range(G):
            for c in range(tm // KC):
                blk_v = vT[g * HEAD_DIM:(g + 1) * HEAD_DIM, c * KC:(c + 1) * KC]
                out_ref[0, g, c] = jnp.concatenate([blk_v, ones_rows], axis=0).astype(BF16)


def _build_kv(x, kvmod, kv_norm_g, w0, wks, wkw, wvsT, wvwT):
    B, S, D = x.shape
    G = N_KV_GROUPS
    tm = min(TOKEN_TILE, S)
    out_shape = (
        jax.ShapeDtypeStruct((B, S, w0.shape[1]), F32),
        jax.ShapeDtypeStruct((B, G, S, QSEL_ROWS), BF16),
        jax.ShapeDtypeStruct((B, G, S, QWIN_ROWS), BF16),
        jax.ShapeDtypeStruct((B, G, S // KC, V_ROWS, KC), BF16),
        jax.ShapeDtypeStruct((B, G, S // KC, V_ROWS, KC), BF16),
    )
    return pl.pallas_call(
        functools.partial(_kv_kernel, tm=tm),
        grid=(B, S // tm),
        in_specs=[
            pl.BlockSpec((1, tm, D), lambda b, i: (b, i, 0)),
            pl.BlockSpec((1, 2, D), lambda b, i: (b, 0, 0)),
            pl.BlockSpec((1, D), lambda b, i: (0, 0)),
            _resident(w0.shape), _resident(wks.shape), _resident(wkw.shape),
            _resident(wvsT.shape), _resident(wvwT.shape),
        ],
        out_specs=(
            pl.BlockSpec((1, tm, w0.shape[1]), lambda b, i: (b, i, 0)),
            pl.BlockSpec((1, G, tm, QSEL_ROWS), lambda b, i: (b, 0, i, 0)),
            pl.BlockSpec((1, G, tm, QWIN_ROWS), lambda b, i: (b, 0, i, 0)),
            pl.BlockSpec((1, G, tm // KC, V_ROWS, KC), lambda b, i: (b, 0, i, 0, 0)),
            pl.BlockSpec((1, G, tm // KC, V_ROWS, KC), lambda b, i: (b, 0, i, 0, 0)),
        ),
        out_shape=out_shape,
        compiler_params=_params(("parallel", "parallel")),
        name="build_kv",
    )(x, kvmod, kv_norm_g.reshape(1, D), w0, wks, wkw, wvsT, wvwT)


def _gelu_tanh(x):
    return 0.5 * x * (1.0 + jnp.tanh(0.7978845608028654 * (x + 0.044715 * (x * x * x))))


def _cmp_kernel(ck_ref, cv_ref, pos_ref, w1_ref, w2k_ref, w2vT_ref, kc_ref, vcT_ref, *, nc):
    def hidden(c, kv):
        a = _dot((c + pos_ref[2 * kv:2 * kv + 1, :]).astype(BF16), w1_ref[kv, 0])
        b = _dot((c + pos_ref[2 * kv + 1:2 * kv + 2, :]).astype(BF16), w1_ref[kv, 1])
        return _gelu_tanh(a + pltpu.roll(b, nc - 1, 0))

    hk = hidden(ck_ref[0, 0], 0).astype(BF16)
    hv = hidden(cv_ref[0, 0], 1).astype(BF16)
    n = lax.broadcasted_iota(jnp.int32, (nc, 1), 0)
    end_hi = (CMP_STRIDE * (n + 1)).astype(F32)
    end_lo = jnp.full((nc, 1), float(CMP_LEN - 1 - CMP_STRIDE), F32)
    kc = _dot(hk, w2k_ref[...]) + _pos_cols((nc, QWIN_ROWS), HEAD_DIM, end_hi, end_lo)
    kc_ref[0, 0] = kc.astype(BF16)
    vcT_ref[0, 0] = _dot_nt(w2vT_ref[...], hv).astype(BF16)


def _compress(ck, cv, pos, w1, w2k, w2vT):
    B, G, nc, F = ck.shape
    return pl.pallas_call(
        functools.partial(_cmp_kernel, nc=nc),
        grid=(B, G),
        in_specs=[
            pl.BlockSpec((1, 1, nc, F), lambda b, g: (b, g, 0, 0)),
            pl.BlockSpec((1, 1, nc, F), lambda b, g: (b, g, 0, 0)),
            pl.BlockSpec(pos.shape, lambda b, g: (0, 0)),
            _resident(w1.shape), _resident(w2k.shape), _resident(w2vT.shape),
        ],
        out_specs=(
            pl.BlockSpec((1, 1, nc, QWIN_ROWS), lambda b, g: (b, g, 0, 0)),
            pl.BlockSpec((1, 1, HEAD_DIM, nc), lambda b, g: (b, g, 0, 0)),
        ),
        out_shape=(
            jax.ShapeDtypeStruct((B, G, nc, QWIN_ROWS), BF16),
            jax.ShapeDtypeStruct((B, G, HEAD_DIM, nc), BF16),
        ),
        compiler_params=_params(("parallel", "parallel")),
        name="compress_kv",
    )(ck, cv, pos, w1, w2k, w2vT)


def _qproj_kernel(x_ref, mod_ref, ng_ref, wqT_ref, wgT_ref, qT_ref, gT_ref):
    h = _pre(x_ref[0], mod_ref, ng_ref, 1).astype(BF16)
    qT_ref[0] = (_dot_nt(wqT_ref[...], h) * (HEAD_DIM ** -0.5 * LOG2E)).astype(BF16)
    gT_ref[0] = jax.nn.sigmoid(_dot_nt(wgT_ref[...], h))


def _qproj(x, mod, ng, wqT, wgT):
    B, S, D = x.shape
    tm = min(TOKEN_TILE, S)
    return pl.pallas_call(
        _qproj_kernel,
        grid=(B, S // tm),
        in_specs=[
            pl.BlockSpec((1, tm, D), lambda b, i: (b, i, 0)),
            pl.BlockSpec((1, 9, D), lambda b, i: (b, 0, 0)),
            pl.BlockSpec((6, D), lambda b, i: (0, 0)),
            _resident(wqT.shape), _resident(wgT.shape),
        ],
        out_specs=(
            pl.BlockSpec((1, wqT.shape[0], tm), lambda b, i: (b, 0, i)),
            pl.BlockSpec((1, wgT.shape[0], tm), lambda b, i: (b, 0, i)),
        ),
        out_shape=(
            jax.ShapeDtypeStruct((B, wqT.shape[0], S), BF16),
            jax.ShapeDtypeStruct((B, wgT.shape[0], S), F32),
        ),
        compiler_params=_params(("parallel", "parallel")),
        name="qproj",
    )(x, mod, ng, wqT, wgT)


def _attn_kernel(qT_ref, gT_ref, kc_ref, vcT_ref, ksel_ref, vselT_ref, kwin_ref, vwinT_ref,
                 arows_ref, ovT_ref, oT_ref, qsel_ref, qwin_ref, m_ref, acc_ref, ocmp_ref, owin_ref,
                 s_buf, p_buf, a_buf, chunks_ref, *, ns, nc, top_n):
    R = HEADS_PER_GROUP
    dh = HEAD_DIM
    qi = pl.program_id(2)
    q0 = qi * TQ
    t_row = q0 + lax.broadcasted_iota(jnp.int32, (1, TQ), 1)

    def lanes(r):
        return slice(r * TQ, (r + 1) * TQ)

    pad_win = jnp.zeros((QWIN_ROWS - dh - ALIBI_ROWS, TQ), BF16)
    pad_sel = jnp.zeros((QSEL_ROWS - dh - SEL_ROWS - ALIBI_ROWS, TQ), BF16)
    for r in range(R):
        qwin_ref[:, lanes(r)] = jnp.concatenate(
            [qT_ref[0, r * dh:(r + 1) * dh, :], arows_ref[0, r], pad_win], axis=0)

    n_io = lax.broadcasted_iota(jnp.int32, (nc, R * TQ), 0)
    t_all = q0 + (lax.broadcasted_iota(jnp.int32, (nc, R * TQ), 1) & (TQ - 1))
    valid_c = (CMP_STRIDE * n_io + (CMP_LEN - 1)) <= t_all
    s = jnp.where(valid_c, _dot(kc_ref[0, 0], qwin_ref[...]), NEG_INF)
    e = jnp.where(valid_c, jnp.exp2(s - jnp.max(s, axis=0, keepdims=True)), 0.0)
    p = e / jnp.maximum(jnp.sum(e, axis=0, keepdims=True), 1e-30)
    o_cmp = _dot(vcT_ref[0, 0], p.astype(BF16))
    psum = p[:, lanes(0)]
    for r in range(1, R):
        psum = psum + p[:, lanes(r)]

    p_hi = psum.astype(BF16)
    rem = psum - p_hi.astype(F32)
    p_mid = rem.astype(BF16)
    p_lo = (rem - p_mid.astype(F32)).astype(BF16)
    ovT = ovT_ref[...]
    imp = _dot(ovT, p_hi) + _dot(ovT, p_mid) + _dot(ovT, p_lo)
    j_io = lax.broadcasted_iota(jnp.int32, (ns, TQ), 0)
    cur = t_row >> SLC_SHIFT
    forced = (j_io == 0) | (j_io == cur) | (j_io == cur - 1)
    future = (j_io * SLC_LEN) > t_row
    imp = jnp.where(forced, FORCE_SCORE, jnp.where(future, NEG_INF, imp))

    ocmp_ref[...] = o_cmp

    k_io = lax.broadcasted_iota(jnp.int32, (KC, R * TQ), 0)
    q_io = lax.broadcasted_iota(jnp.int32, (KC, R * TQ), 1) & (TQ - 1)
    causal = k_io <= q_io
    band = k_io > q_io

    def win_scores(c):
        start = pl.multiple_of(c * KC, KC)
        return _dot(kwin_ref[0, 0, pl.ds(start, KC), :], qwin_ref[...])

    c_mid = jnp.maximum(qi - 1, 0)
    c_far = jnp.maximum(qi - 2, 0)
    s_near = jnp.where(causal, win_scores(qi), NEG_INF)
    s_mid = jnp.where(qi >= 1, win_scores(c_mid), NEG_INF)
    s_far = jnp.where(band & (qi >= 2), win_scores(c_far), NEG_INF)
    m_win = jnp.maximum(jnp.max(s_near, axis=0, keepdims=True),
                        jnp.maximum(jnp.max(s_mid, axis=0, keepdims=True),
                                    jnp.max(s_far, axis=0, keepdims=True)))
    acc_win = (_dot(vwinT_ref[0, 0, qi], jnp.exp2(s_near - m_win).astype(BF16))
               + _dot(vwinT_ref[0, 0, c_mid], jnp.exp2(s_mid - m_win).astype(BF16))
               + _dot(vwinT_ref[0, 0, c_far], jnp.exp2(s_far - m_win).astype(BF16)))
    owin_ref[...] = acc_win[:dh, :] / acc_win[dh:dh + 1, :]

    groups = [imp[8 * k:8 * k + 8, :] for k in range(ns // 8)]
    ranks = [jnp.zeros((8, TQ), jnp.int32) for _ in groups]
    sub_io = lax.broadcasted_iota(jnp.int32, (8, TQ), 0)
    for jp in range(ns):
        row = imp[jp:jp + 1, :]
        for k, grp in enumerate(groups):
            if 8 * k > jp:
                beats = (row >= grp).astype(jnp.int32)
            elif 8 * k + 7 < jp:
                beats = (row > grp).astype(jnp.int32)
            else:
                beats = jnp.where(sub_io > jp - 8 * k, (row >= grp).astype(jnp.int32),
                                  (row > grp).astype(jnp.int32))
            ranks[k] = ranks[k] + beats
    rank = jnp.concatenate(ranks, axis=0)
    sel_rows = jnp.where(rank < top_n, 0.0, -MASK_BIG).astype(BF16)

    row_used = jnp.max(jnp.where(rank < top_n, 1.0, 0.0), axis=1, keepdims=True)
    blocks_per_chunk = KC // SLC_LEN
    n_full = jnp.int32(0)
    for c in range(ns // blocks_per_chunk):
        used = jnp.max(row_used[c * blocks_per_chunk:(c + 1) * blocks_per_chunk, :]) > 0.0
        chunks_ref[n_full] = c
        n_full = n_full + jnp.where(used & (c < qi), 1, 0)
    chunks_ref[n_full] = qi
    if ns < SEL_ROWS:
        sel_rows = jnp.concatenate([sel_rows, jnp.zeros((SEL_ROWS - ns, TQ), BF16)], axis=0)
    for r in range(R):
        qsel_ref[:, lanes(r)] = jnp.concatenate(
            [qT_ref[0, r * dh:(r + 1) * dh, :], sel_rows, arows_ref[0, r], pad_sel], axis=0)

    def sel_scores(c):
        start = pl.multiple_of(c * KC, KC)
        return _dot(ksel_ref[0, 0, pl.ds(start, KC), :], qsel_ref[...])

    def sel_softmax(slot, mask):
        sc = s_buf[slot]
        if mask is not None:
            sc = jnp.where(mask, sc, NEG_INF)
        m_old = m_ref[...]
        m_new = jnp.maximum(m_old, jnp.max(sc, axis=0, keepdims=True))
        p_buf[slot] = jnp.exp2(sc - m_new).astype(BF16)
        a_buf[slot] = jnp.exp2(m_old - m_new)
        m_ref[...] = m_new

    def sel_values(c, slot):
        acc_ref[...] = a_buf[slot] * acc_ref[...] + _dot(vselT_ref[0, 0, c], p_buf[slot])

    m_ref[...] = jnp.full(m_ref.shape, NEG_INF, F32)
    acc_ref[...] = jnp.zeros(acc_ref.shape, F32)
    p_buf[1] = jnp.zeros(p_buf.shape[1:], BF16)
    a_buf[1] = jnp.ones(a_buf.shape[1:], F32)
    def chunk_at(k):
        return chunks_ref[jnp.maximum(k, 0)]

    s_buf[0] = sel_scores(chunk_at(0))

    def sel_pair(i, carry):
        k = 2 * i
        s_buf[1] = sel_scores(chunk_at(k + 1))
        sel_softmax(0, None)
        sel_values(chunk_at(k - 1), 1)
        s_buf[0] = sel_scores(chunk_at(k + 2))
        sel_softmax(1, None)
        sel_values(chunk_at(k), 0)
        return carry

    lax.fori_loop(0, n_full >> 1, sel_pair, 0)

    @pl.when((n_full & 1) == 1)
    def _():
        sel_softmax(0, None)
        s_buf[1] = sel_scores(qi)
        sel_values(chunk_at(n_full - 2), 1)
        sel_softmax(1, causal)
        sel_values(chunk_at(n_full - 1), 0)
        sel_values(qi, 1)

    @pl.when((n_full & 1) == 0)
    def _():
        sel_softmax(0, causal)
        sel_values(chunk_at(n_full - 1), 1)
        sel_values(qi, 0)

    acc = acc_ref[...]
    o_sel = acc[:dh, :] / acc[dh:dh + 1, :]
    for r in range(R):
        o = (gT_ref[0, r:r + 1, :] * ocmp_ref[:, lanes(r)]
             + gT_ref[0, R + r:R + r + 1, :] * o_sel[:, lanes(r)]
             + gT_ref[0, 2 * R + r:2 * R + r + 1, :] * owin_ref[:, lanes(r)])
        oT_ref[0, r * dh:(r + 1) * dh, :] = o.astype(BF16)


def _attention(qT, gT, kc, vcT, ksel, vselT, kwin, vwinT, arows, ovT):
    B, HD, S = qT.shape
    G, R, dh = N_KV_GROUPS, HEADS_PER_GROUP, HEAD_DIM
    nc = kc.shape[2]
    ns = S // SLC_LEN
    n_kc = S // KC
    return pl.pallas_call(
        functools.partial(_attn_kernel, ns=ns, nc=nc, top_n=min(SLC_TOP_N, ns)),
        grid=(B, G, S // TQ),
        in_specs=[
            pl.BlockSpec((1, R * dh, TQ), lambda b, g, i: (b, g, i)),
            pl.BlockSpec((1, gT.shape[1] // G, TQ), lambda b, g, i: (b, g, i)),
            pl.BlockSpec((1, 1, nc, QWIN_ROWS), lambda b, g, i: (b, g, 0, 0)),
            pl.BlockSpec((1, 1, dh, nc), lambda b, g, i: (b, g, 0, 0)),
            pl.BlockSpec((1, 1, S, QSEL_ROWS), lambda b, g, i: (b, g, 0, 0)),
            pl.BlockSpec((1, 1, n_kc, V_ROWS, KC), lambda b, g, i: (b, g, 0, 0, 0)),
            pl.BlockSpec((1, 1, S, QWIN_ROWS), lambda b, g, i: (b, g, 0, 0)),
            pl.BlockSpec((1, 1, n_kc, V_ROWS, KC), lambda b, g, i: (b, g, 0, 0, 0)),
            pl.BlockSpec((1, R, ALIBI_ROWS, TQ), lambda b, g, i: (g, 0, 0, 0)),
            pl.BlockSpec(ovT.shape, lambda b, g, i: (0, 0)),
        ],
        out_specs=pl.BlockSpec((1, R * dh, TQ), lambda b, g, i: (b, g, i)),
        out_shape=jax.ShapeDtypeStruct((B, HD, S), BF16),
        scratch_shapes=[
            pltpu.VMEM((QSEL_ROWS, R * TQ), BF16),
            pltpu.VMEM((QWIN_ROWS, R * TQ), BF16),
            pltpu.VMEM((1, R * TQ), F32),
            pltpu.VMEM((V_ROWS, R * TQ), F32),
            pltpu.VMEM((dh, R * TQ), F32),
            pltpu.VMEM((dh, R * TQ), F32),
            pltpu.VMEM((2, KC, R * TQ), F32),
            pltpu.VMEM((2, KC, R * TQ), BF16),
            pltpu.VMEM((2, 1, R * TQ), F32),
            pltpu.SMEM((n_kc + 1,), jnp.int32),
        ],
        compiler_params=_params(("parallel", "parallel", "arbitrary")),
        name="nsa_attention",
    )(qT, gT, kc, vcT, ksel, vselT, kwin, vwinT, arows, ovT)


def _oproj_kernel(x_ref, oT_ref, mod_ref, ng_ref, w_ref, o_ref):
    y = _dot_tn(oT_ref[0], w_ref[...])
    o_ref[0] = _post(x_ref[0], y, mod_ref, ng_ref, 1, 1.0)


def _oproj(x, oT, mod, ng, w_out):
    B, S, D = x.shape
    tm = min(TOKEN_TILE, S)
    return pl.pallas_call(
        _oproj_kernel,
        grid=(B, S // tm),
        in_specs=[
            pl.BlockSpec((1, tm, D), lambda b, i: (b, i, 0)),
            pl.BlockSpec((1, oT.shape[1], tm), lambda b, i: (b, 0, i)),
            pl.BlockSpec((1, 9, D), lambda b, i: (b, 0, 0)),
            pl.BlockSpec((6, D), lambda b, i: (0, 0)),
            _resident(w_out.shape),
        ],
        out_specs=pl.BlockSpec((1, tm, D), lambda b, i: (b, i, 0)),
        out_shape=jax.ShapeDtypeStruct(x.shape, F32),
        compiler_params=_params(("parallel", "parallel")),
        name="oproj",
    )(x, oT, mod, ng, w_out)


def _split3(v):
    hi = v.astype(BF16)
    r1 = v - hi.astype(F32)
    mid = r1.astype(BF16)
    lo = (r1 - mid.astype(F32)).astype(BF16)
    return hi, mid, lo


def _alibi_rows():
    h = jnp.arange(1, N_HEADS + 1, dtype=F32)
    slopes = jnp.exp2(-8.0 * h / N_HEADS).reshape(N_KV_GROUPS, HEADS_PER_GROUP)
    parts = _split3(slopes * LOG2E)
    rows = jnp.stack(list(parts) * 2 + [jnp.zeros_like(parts[0])] * (ALIBI_ROWS - 6), axis=-1)
    return jnp.broadcast_to(rows[..., None], rows.shape + (TQ,))


def _overlap_t(ns, nc, n_cmp):
    i = jnp.arange(nc)[None, :]
    j = jnp.arange(ns)[:, None]
    start = i * CMP_STRIDE
    ov = (start < (j + 1) * SLC_LEN) & (start + CMP_LEN - 1 >= j * SLC_LEN) & (i < n_cmp)
    return ov.astype(BF16)


def _pad_cols(w, width):
    return jnp.pad(w, ((0, 0), (0, width - w.shape[1])))


def kernel(x, c, ada_w, ada_b, norm_g, ffn_w_in, ffn_w_out, a_w_in, a_conv, a_w_out,
           kv_norm_g, kv_ada_w, kv_ada_b, kv_w, cmp_pos, cmp_w1, cmp_w2, b_w_in, b_w_out):
    B, S, D = x.shape
    G, R, dh = N_KV_GROUPS, HEADS_PER_GROUP, HEAD_DIM
    n_a = DEPTH // 2
    assert S % TQ == 0 and WINDOW == 2 * KC and TQ == KC
    assert D == N_HEADS * dh

    mods = _ada(c, ada_w, ada_b, tn=2304).reshape(DEPTH, B, 9, D)
    kvmod = _ada(c, kv_ada_w[None], kv_ada_b[None], tn=1024).reshape(B, 2, D)

    arows = _alibi_rows()
    nc = S // CMP_STRIDE
    n_cmp = (S - CMP_LEN) // CMP_STRIDE + 1
    ovT = _overlap_t(S // SLC_LEN, nc, n_cmp)

    kv_state = None
    for layer in range(DEPTH):
        mod = mods[layer]
        ng = norm_g[layer].reshape(6, D)
        if layer == n_a:
            gcols = G * dh
            w0 = kv_w[:, :2 * gcols].astype(BF16)
            k_sel = kv_w[:, 2 * gcols:3 * gcols]
            k_win = kv_w[:, 4 * gcols:5 * gcols]
            wks = jnp.concatenate(
                [_pad_cols(k_sel[:, g * dh:(g + 1) * dh], QSEL_ROWS) for g in range(G)], axis=1)
            wkw = jnp.concatenate(
                [_pad_cols(k_win[:, g * dh:(g + 1) * dh], QWIN_ROWS) for g in range(G)], axis=1)
            wvsT = kv_w[:, 3 * gcols:4 * gcols].T.astype(BF16)
            wvwT = kv_w[:, 5 * gcols:6 * gcols].T.astype(BF16)
            kv0, ksel, kwin, vselT, vwinT = _build_kv(
                x, kvmod, kv_norm_g, w0, wks.astype(BF16), wkw.astype(BF16), wvsT, wvwT)

            def chunked(a):
                a = a.reshape(B, S, G, dh).transpose(0, 2, 1, 3)
                return a.reshape(B, G, nc, CMP_STRIDE * dh)

            half = CMP_STRIDE * dh
            kc, vcT = _compress(
                chunked(kv0[..., :gcols]), chunked(kv0[..., gcols:]),
                cmp_pos.reshape(4, half),
                cmp_w1.reshape(2, 2, half, cmp_w1.shape[-1]).astype(BF16),
                _pad_cols(cmp_w2[0], QWIN_ROWS).astype(BF16),
                cmp_w2[1].T.astype(BF16))
            kv_state = (kc, vcT, ksel, vselT, kwin, vwinT)

        x = _ffn(x, mod, ng, ffn_w_in[layer, 0].astype(BF16), ffn_w_out[layer, 0].astype(BF16), 0)
        if layer < n_a:
            x = _conv_mixer(x, mod, ng, a_w_in[layer].astype(BF16), a_conv[layer],
                            a_w_out[layer].astype(BF16))
        else:
            w_in = b_w_in[layer - n_a]
            wqT = w_in[:, :N_HEADS * dh].T.astype(BF16)
            wg = w_in[:, N_HEADS * dh:].reshape(D, N_BRANCH, G, R)
            wg = jnp.pad(wg.transpose(2, 1, 3, 0), ((0, 0), (0, 1), (0, 0), (0, 0)))
            wgT = wg.reshape(G * (N_BRANCH + 1) * R, D).astype(BF16)
            qT, gT = _qproj(x, mod, ng, wqT, wgT)
            kc, vcT, ksel, vselT, kwin, vwinT = kv_state
            oT = _attention(qT, gT, kc, vcT, ksel, vselT, kwin, vwinT, arows, ovT)
            x = _oproj(x, oT, mod, ng, b_w_out[layer - n_a].astype(BF16))
        x = _ffn(x, mod, ng, ffn_w_in[layer, 1].astype(BF16), ffn_w_out[layer, 1].astype(BF16), 2)
    return x
```

```python
import functools

import jax
import jax.numpy as jnp
from jax import lax
from jax.experimental import pallas as pl
from jax.experimental.pallas import tpu as pltpu

F32 = jnp.float32
BF16 = jnp.bfloat16

N_HEADS = 16
N_KV_GROUPS = 4
HEADS_PER_GROUP = N_HEADS // N_KV_GROUPS
HEAD_DIM = 64
N_BRANCH = 3
CMP_LEN = 32
CMP_STRIDE = 16
SLC_LEN = 64
SLC_SHIFT = 6
SLC_TOP_N = 16
WINDOW = 512
CONV_WIDTH = 3
RMS_EPS = 1e-6
NEG_INF = -1e30
FORCE_SCORE = 1e30
DEPTH = 4

TQ = 256
KC = 256
SEL_ROWS = 64
ALIBI_ROWS = 16
QSEL_ROWS = 256
QWIN_ROWS = 128
V_ROWS = HEAD_DIM + 16
MASK_BIG = float(2 ** 30)
LOG2E = 1.4426950408889634

MXU_COLS = 256
VMEM_LIMIT = 56 * 1024 * 1024
TOKEN_TILE = 512
FFN_TOKEN_TILE = 1024
FFN_ROW_SPLITS = 4


def _params(sem):
    return pltpu.CompilerParams(dimension_semantics=sem, vmem_limit_bytes=VMEM_LIMIT)


def _resident(shape):
    nd = len(shape)
    return pl.BlockSpec(shape, lambda *_: (0,) * nd, pipeline_mode=pl.Buffered(1))


def _rms(x, g):
    ms = jnp.mean(x * x, axis=-1, keepdims=True)
    return x * lax.rsqrt(ms + RMS_EPS) * g


def _pre(x, mod_ref, ng_ref, sub):
    shift = mod_ref[0, 3 * sub:3 * sub + 1, :]
    scale = mod_ref[0, 3 * sub + 1:3 * sub + 2, :]
    return _rms(x, ng_ref[2 * sub:2 * sub + 1, :]) * (1.0 + scale) + shift


def _post(x, y, mod_ref, ng_ref, sub, weight):
    gate = mod_ref[0, 3 * sub + 2:3 * sub + 3, :]
    return x + weight * gate * _rms(y, ng_ref[2 * sub + 1:2 * sub + 2, :])


def _dot(a, b):
    return jnp.dot(a, b, preferred_element_type=F32)


def _dot_nt(a, b):
    return lax.dot_general(a, b, (((1,), (1,)), ((), ())), preferred_element_type=F32)


def _dot_tn(a, b):
    return lax.dot_general(a, b, (((0,), (0,)), ((), ())), preferred_element_type=F32)


def _ada_kernel(c_ref, w_ref, b_ref, o_ref):
    c = c_ref[...]
    sc = c * jax.nn.sigmoid(c)
    o_ref[0] = _dot(sc, w_ref[0]) + b_ref[0]


def _ada(c, w, b, tn):
    L, D, N = w.shape
    B = c.shape[0]
    return pl.pallas_call(
        _ada_kernel,
        grid=(L, N // tn),
        in_specs=[
            pl.BlockSpec((B, D), lambda l, j: (0, 0)),
            pl.BlockSpec((1, D, tn), lambda l, j: (l, 0, j)),
            pl.BlockSpec((1, 1, tn), lambda l, j: (l, 0, j)),
        ],
        out_specs=pl.BlockSpec((1, B, tn), lambda l, j: (l, 0, j)),
        out_shape=jax.ShapeDtypeStruct((L, B, N), F32),
        compiler_params=_params(("parallel", "parallel")),
        name="ada",
    )(c, w, b.reshape(L, 1, N))


def _ffn_kernel(x_ref, mod_ref, ng_ref, win_ref, wout_ref, o_ref, *, sub, d_ff, n_chunks):
    tf = d_ff // n_chunks
    rows = x_ref.shape[1] // FFN_ROW_SPLITS
    for i in range(FFN_ROW_SPLITS):
        x = x_ref[0, i * rows:(i + 1) * rows, :]
        h = _pre(x, mod_ref, ng_ref, sub).astype(BF16)
        y = None
        for c in range(n_chunks):
            g = _dot(h, win_ref[:, c * tf:(c + 1) * tf])
            u = _dot(h, win_ref[:, d_ff + c * tf:d_ff + (c + 1) * tf])
            a = (g * jax.nn.sigmoid(g) * u).astype(BF16)
            part = _dot(a, wout_ref[c * tf:(c + 1) * tf, :])
            y = part if y is None else y + part
        o_ref[0, i * rows:(i + 1) * rows, :] = _post(x, y, mod_ref, ng_ref, sub, 0.5)


def _oproj_ffn_kernel(x_ref, oT_ref, mod_ref, ng_ref, wo_ref, win_ref, wout_ref, o_ref, *, d_ff):
    rows = x_ref.shape[1] // FFN_ROW_SPLITS
    for i in range(FFN_ROW_SPLITS):
        x = x_ref[0, i * rows:(i + 1) * rows, :]
        y = _dot_tn(oT_ref[0, :, i * rows:(i + 1) * rows], wo_ref[...])
        x = _post(x, y, mod_ref, ng_ref, 1, 1.0)
        h = _pre(x, mod_ref, ng_ref, 2).astype(BF16)
        g = _dot(h, win_ref[:, :d_ff])
        u = _dot(h, win_ref[:, d_ff:])
        a = (g * jax.nn.sigmoid(g) * u).astype(BF16)
        o_ref[0, i * rows:(i + 1) * rows, :] = _post(
            x, _dot(a, wout_ref[...]), mod_ref, ng_ref, 2, 0.5)


def _oproj_ffn(x, oT, mod, ng, w_o, w_in, w_out):
    B, S, D = x.shape
    tm = min(FFN_TOKEN_TILE, S)
    return pl.pallas_call(
        functools.partial(_oproj_ffn_kernel, d_ff=w_out.shape[0]),
        grid=(B, S // tm),
        in_specs=[
            pl.BlockSpec((1, tm, D), lambda b, i: (b, i, 0)),
            pl.BlockSpec((1, oT.shape[1], tm), lambda b, i: (b, 0, i)),
            pl.BlockSpec((1, 9, D), lambda b, i: (b, 0, 0)),
            pl.BlockSpec((6, D), lambda b, i: (0, 0)),
            _resident(w_o.shape), _resident(w_in.shape), _resident(w_out.shape),
        ],
        out_specs=pl.BlockSpec((1, tm, D), lambda b, i: (b, i, 0)),
        out_shape=jax.ShapeDtypeStruct(x.shape, F32),
        compiler_params=_params(("parallel", "parallel")),
        name="oproj_ffn",
    )(x, oT, mod, ng, w_o, w_in, w_out)


def _ffn(x, mod, ng, w_in, w_out, sub):
    B, S, D = x.shape
    d_ff = w_out.shape[0]
    tm = min(FFN_TOKEN_TILE, S)
    n_chunks = next(n for n in (1, 2, 4) if (d_ff // n) % MXU_COLS == 0 and d_ff % n == 0)
    return pl.pallas_call(
        functools.partial(_ffn_kernel, sub=sub, d_ff=d_ff, n_chunks=n_chunks),
        grid=(B, S // tm),
        in_specs=[
            pl.BlockSpec((1, tm, D), lambda b, i: (b, i, 0)),
            pl.BlockSpec((1, 9, D), lambda b, i: (b, 0, 0)),
            pl.BlockSpec((6, D), lambda b, i: (0, 0)),
            _resident(w_in.shape),
            _resident(w_out.shape),
        ],
        out_specs=pl.BlockSpec((1, tm, D), lambda b, i: (b, i, 0)),
        out_shape=jax.ShapeDtypeStruct(x.shape, F32),
        compiler_params=_params(("parallel", "parallel")),
        name="ffn",
    )(x, mod, ng, w_in, w_out)


def _conv_kernel(x_ref, mod_ref, ng_ref, win_ref, cw_ref, wout_ref, o_ref, vbuf, *, tm):
    D = x_ref.shape[-1]
    x = x_ref[0]
    h = _pre(x, mod_ref, ng_ref, 1).astype(BF16)
    proj = _dot(h, win_ref[...])
    b_gate = proj[:, :D]
    v = proj[:, D:2 * D] * proj[:, 2 * D:]

    @pl.when(pl.program_id(1) == 0)
    def _():
        vbuf[0:8, :] = jnp.zeros((8, D), F32)

    vbuf[8:8 + tm, :] = v
    y = (cw_ref[2:3, :] * v
         + cw_ref[1:2, :] * vbuf[7:7 + tm, :]
         + cw_ref[0:1, :] * vbuf[6:6 + tm, :])
    vbuf[0:8, :] = v[tm - 8:, :]
    z = (b_gate * y).astype(BF16)
    o_ref[0] = _post(x, _dot(z, wout_ref[...]), mod_ref, ng_ref, 1, 1.0)


def _conv_mixer(x, mod, ng, w_in, conv_w, w_out):
    B, S, D = x.shape
    tm = min(TOKEN_TILE, S)
    return pl.pallas_call(
        functools.partial(_conv_kernel, tm=tm),
        grid=(B, S // tm),
        in_specs=[
            pl.BlockSpec((1, tm, D), lambda b, i: (b, i, 0)),
            pl.BlockSpec((1, 9, D), lambda b, i: (b, 0, 0)),
            pl.BlockSpec((6, D), lambda b, i: (0, 0)),
            _resident(w_in.shape),
            pl.BlockSpec(conv_w.shape, lambda b, i: (0, 0)),
            _resident(w_out.shape),
        ],
        out_specs=pl.BlockSpec((1, tm, D), lambda b, i: (b, i, 0)),
        out_shape=jax.ShapeDtypeStruct(x.shape, F32),
        scratch_shapes=[pltpu.VMEM((tm + 8, D), F32)],
        compiler_params=_params(("parallel", "arbitrary")),
        name="conv_mixer",
    )(x, mod, ng, w_in, conv_w, w_out)


def _pos_cols(shape, first, hi, lo):
    lane = lax.broadcasted_iota(jnp.int32, shape, 1)
    hi_b = jnp.broadcast_to(hi, shape)
    lo_b = jnp.broadcast_to(lo, shape)
    return jnp.where((lane >= first) & (lane < first + 3), hi_b,
                     jnp.where((lane >= first + 3) & (lane < first + 6), lo_b, 0.0))


def _kv_kernel(x_ref, mod_ref, g_ref, w0_ref, wks_ref, wkw_ref, wvsT_ref, wvwT_ref,
               kv0_ref, ksel_ref, kwin_ref, vselT_ref, vwinT_ref, *, tm):
    G = N_KV_GROUPS
    x = x_ref[0]
    h = (_rms(x, g_ref[...]) * (1.0 + mod_ref[0, 1:2, :]) + mod_ref[0, 0:1, :]).astype(BF16)
    kv0_ref[0] = _dot(h, w0_ref[...])

    tok = pl.program_id(1) * tm + lax.broadcasted_iota(jnp.int32, (tm, 1), 0)
    blk = tok >> SLC_SHIFT
    blk64 = (blk * SLC_LEN).astype(F32)
    off = (tok & (SLC_LEN - 1)).astype(F32)
    lane = lax.broadcasted_iota(jnp.int32, (tm, QSEL_ROWS), 1)
    onehot = ((lane - HEAD_DIM) == blk).astype(F32)
    sel_const = onehot + _pos_cols((tm, QSEL_ROWS), HEAD_DIM + SEL_ROWS, blk64, off)
    win_const = _pos_cols((tm, QWIN_ROWS), HEAD_DIM, blk64, off)

    ks = _dot(h, wks_ref[...])
    kw = _dot(h, wkw_ref[...])
    for g in range(G):
        ksel_ref[0, g] = (ks[:, g * QSEL_ROWS:(g + 1) * QSEL_ROWS] + sel_const).astype(BF16)
        kwin_ref[0, g] = (kw[:, g * QWIN_ROWS:(g + 1) * QWIN_ROWS] + win_const).astype(BF16)

    ones_rows = (lax.broadcasted_iota(jnp.int32, (V_ROWS - HEAD_DIM, KC), 0) == 0).astype(F32)
    for wT_ref, out_ref in ((wvsT_ref, vselT_ref), (wvwT_ref, vwinT_ref)):
        vT = _dot_nt(wT_ref[...], h)
        for g in range(G):
            for c in range(tm // KC):
                blk_v = vT[g * HEAD_DIM:(g + 1) * HEAD_DIM, c * KC:(c + 1) * KC]
                out_ref[0, g, c] = jnp.concatenate([blk_v, ones_rows], axis=0).astype(BF16)


def _build_kv(x, kvmod, kv_norm_g, w0, wks, wkw, wvsT, wvwT):
    B, S, D = x.shape
    G = N_KV_GROUPS
    tm = min(TOKEN_TILE, S)
    out_shape = (
        jax.ShapeDtypeStruct((B, S, w0.shape[1]), F32),
        jax.ShapeDtypeStruct((B, G, S, QSEL_ROWS), BF16),
        jax.ShapeDtypeStruct((B, G, S, QWIN_ROWS), BF16),
        jax.ShapeDtypeStruct((B, G, S // KC, V_ROWS, KC), BF16),
        jax.ShapeDtypeStruct((B, G, S // KC, V_ROWS, KC), BF16),
    )
    return pl.pallas_call(
        functools.partial(_kv_kernel, tm=tm),
        grid=(B, S // tm),
        in_specs=[
            pl.BlockSpec((1, tm, D), lambda b, i: (b, i, 0)),
            pl.BlockSpec((1, 2, D), lambda b, i: (b, 0, 0)),
            pl.BlockSpec((1, D), lambda b, i: (0, 0)),
            _resident(w0.shape), _resident(wks.shape), _resident(wkw.shape),
            _resident(wvsT.shape), _resident(wvwT.shape),
        ],
        out_specs=(
            pl.BlockSpec((1, tm, w0.shape[1]), lambda b, i: (b, i, 0)),
            pl.BlockSpec((1, G, tm, QSEL_ROWS), lambda b, i: (b, 0, i, 0)),
            pl.BlockSpec((1, G, tm, QWIN_ROWS), lambda b, i: (b, 0, i, 0)),
            pl.BlockSpec((1, G, tm // KC, V_ROWS, KC), lambda b, i: (b, 0, i, 0, 0)),
            pl.BlockSpec((1, G, tm // KC, V_ROWS, KC), lambda b, i: (b, 0, i, 0, 0)),
        ),
        out_shape=out_shape,
        compiler_params=_params(("parallel", "parallel")),
        name="build_kv",
    )(x, kvmod, kv_norm_g.reshape(1, D), w0, wks, wkw, wvsT, wvwT)


def _gelu_tanh(x):
    return 0.5 * x * (1.0 + jnp.tanh(0.7978845608028654 * (x + 0.044715 * (x * x * x))))


def _cmp_kernel(ck_ref, cv_ref, pos_ref, w1_ref, w2k_ref, w2vT_ref, kc_ref, vcT_ref, *, nc):
    def hidden(c, kv):
        a = _dot((c + pos_ref[2 * kv:2 * kv + 1, :]).astype(BF16), w1_ref[kv, 0])
        b = _dot((c + pos_ref[2 * kv + 1:2 * kv + 2, :]).astype(BF16), w1_ref[kv, 1])
        return _gelu_tanh(a + pltpu.roll(b, nc - 1, 0))

    hk = hidden(ck_ref[0, 0], 0).astype(BF16)
    hv = hidden(cv_ref[0, 0], 1).astype(BF16)
    n = lax.broadcasted_iota(jnp.int32, (nc, 1), 0)
    end_hi = (CMP_STRIDE * (n + 1)).astype(F32)
    end_lo = jnp.full((nc, 1), float(CMP_LEN - 1 - CMP_STRIDE), F32)
    kc = _dot(hk, w2k_ref[...]) + _pos_cols((nc, QWIN_ROWS), HEAD_DIM, end_hi, end_lo)
    kc_ref[0, 0] = kc.astype(BF16)
    vcT_ref[0, 0] = _dot_nt(w2vT_ref[...], hv).astype(BF16)


def _compress(ck, cv, pos, w1, w2k, w2vT):
    B, G, nc, F = ck.shape
    return pl.pallas_call(
        functools.partial(_cmp_kernel, nc=nc),
        grid=(B, G),
        in_specs=[
            pl.BlockSpec((1, 1, nc, F), lambda b, g: (b, g, 0, 0)),
            pl.BlockSpec((1, 1, nc, F), lambda b, g: (b, g, 0, 0)),
            pl.BlockSpec(pos.shape, lambda b, g: (0, 0)),
            _resident(w1.shape), _resident(w2k.shape), _resident(w2vT.shape),
        ],
        out_specs=(
            pl.BlockSpec((1, 1, nc, QWIN_ROWS), lambda b, g: (b, g, 0, 0)),
            pl.BlockSpec((1, 1, HEAD_DIM, nc), lambda b, g: (b, g, 0, 0)),
        ),
        out_shape=(
            jax.ShapeDtypeStruct((B, G, nc, QWIN_ROWS), BF16),
            jax.ShapeDtypeStruct((B, G, HEAD_DIM, nc), BF16),
        ),
        compiler_params=_params(("parallel", "parallel")),
        name="compress_kv",
    )(ck, cv, pos, w1, w2k, w2vT)


def _qproj_kernel(x_ref, mod_ref, ng_ref, wqT_ref, wgT_ref, qT_ref, gT_ref):
    h = _pre(x_ref[0], mod_ref, ng_ref, 1).astype(BF16)
    qT_ref[0] = (_dot_nt(wqT_ref[...], h) * (HEAD_DIM ** -0.5 * LOG2E)).astype(BF16)
    gT_ref[0] = jax.nn.sigmoid(_dot_nt(wgT_ref[...], h))


def _qproj(x, mod, ng, wqT, wgT):
    B, S, D = x.shape
    tm = min(TOKEN_TILE, S)
    return pl.pallas_call(
        _qproj_kernel,
        grid=(B, S // tm),
        in_specs=[
            pl.BlockSpec((1, tm, D), lambda b, i: (b, i, 0)),
            pl.BlockSpec((1, 9, D), lambda b, i: (b, 0, 0)),
            pl.BlockSpec((6, D), lambda b, i: (0, 0)),
            _resident(wqT.shape), _resident(wgT.shape),
        ],
        out_specs=(
            pl.BlockSpec((1, wqT.shape[0], tm), lambda b, i: (b, 0, i)),
            pl.BlockSpec((1, wgT.shape[0], tm), lambda b, i: (b, 0, i)),
        ),
        out_shape=(
            jax.ShapeDtypeStruct((B, wqT.shape[0], S), BF16),
            jax.ShapeDtypeStruct((B, wgT.shape[0], S), F32),
        ),
        compiler_params=_params(("parallel", "parallel")),
        name="qproj",
    )(x, mod, ng, wqT, wgT)


def _attn_kernel(qT_ref, gT_ref, kc_ref, vcT_ref, ksel_ref, vselT_ref, kwin_ref, vwinT_ref,
                 arows_ref, ovT_ref, oT_ref, qsel_ref, qwin_ref, m_ref, acc_ref, ocmp_ref, owin_ref,
                 s_buf, p_buf, a_buf, chunks_ref, *, ns, nc, top_n):
    R = HEADS_PER_GROUP
    dh = HEAD_DIM
    qi = pl.program_id(2)
    q0 = qi * TQ
    t_row = q0 + lax.broadcasted_iota(jnp.int32, (1, TQ), 1)

    def lanes(r):
        return slice(r * TQ, (r + 1) * TQ)

    pad_win = jnp.zeros((QWIN_ROWS - dh - ALIBI_ROWS, TQ), BF16)
    pad_sel = jnp.zeros((QSEL_ROWS - dh - SEL_ROWS - ALIBI_ROWS, TQ), BF16)
    for r in range(R):
        qwin_ref[:, lanes(r)] = jnp.concatenate(
            [qT_ref[0, r * dh:(r + 1) * dh, :], arows_ref[0, r], pad_win], axis=0)

    n_io = lax.broadcasted_iota(jnp.int32, (nc, R * TQ), 0)
    t_all = q0 + (lax.broadcasted_iota(jnp.int32, (nc, R * TQ), 1) & (TQ - 1))
    valid_c = (CMP_STRIDE * n_io + (CMP_LEN - 1)) <= t_all
    s = jnp.where(valid_c, _dot(kc_ref[0, 0], qwin_ref[...]), NEG_INF)
    e = jnp.where(valid_c, jnp.exp2(s - jnp.max(s, axis=0, keepdims=True)), 0.0)
    p = e / jnp.maximum(jnp.sum(e, axis=0, keepdims=True), 1e-30)
    o_cmp = _dot(vcT_ref[0, 0], p.astype(BF16))
    psum = p[:, lanes(0)]
    for r in range(1, R):
        psum = psum + p[:, lanes(r)]

    p_hi = psum.astype(BF16)
    rem = psum - p_hi.astype(F32)
    p_mid = rem.astype(BF16)
    p_lo = (rem - p_mid.astype(F32)).astype(BF16)
    ovT = ovT_ref[...]
    imp = _dot(ovT, p_hi) + _dot(ovT, p_mid) + _dot(ovT, p_lo)
    j_io = lax.broadcasted_iota(jnp.int32, (ns, TQ), 0)
    cur = t_row >> SLC_SHIFT
    forced = (j_io == 0) | (j_io == cur) | (j_io == cur - 1)
    future = (j_io * SLC_LEN) > t_row
    imp = jnp.where(forced, FORCE_SCORE, jnp.where(future, NEG_INF, imp))

    ocmp_ref[...] = o_cmp

    k_io = lax.broadcasted_iota(jnp.int32, (KC, R * TQ), 0)
    q_io = lax.broadcasted_iota(jnp.int32, (KC, R * TQ), 1) & (TQ - 1)
    causal = k_io <= q_io
    band = k_io > q_io

    def win_scores(c):
        start = pl.multiple_of(c * KC, KC)
        return _dot(kwin_ref[0, 0, pl.ds(start, KC), :], qwin_ref[...])

    c_mid = jnp.maximum(qi - 1, 0)
    c_far = jnp.maximum(qi - 2, 0)
    s_near = jnp.where(causal, win_scores(qi), NEG_INF)
    s_mid = jnp.where(qi >= 1, win_scores(c_mid), NEG_INF)
    s_far = jnp.where(band & (qi >= 2), win_scores(c_far), NEG_INF)
    m_win = jnp.maximum(jnp.max(s_near, axis=0, keepdims=True),
                        jnp.maximum(jnp.max(s_mid, axis=0, keepdims=True),
                                    jnp.max(s_far, axis=0, keepdims=True)))
    acc_win = (_dot(vwinT_ref[0, 0, qi], jnp.exp2(s_near - m_win).astype(BF16))
               + _dot(vwinT_ref[0, 0, c_mid], jnp.exp2(s_mid - m_win).astype(BF16))
               + _dot(vwinT_ref[0, 0, c_far], jnp.exp2(s_far - m_win).astype(BF16)))
    owin_ref[...] = acc_win[:dh, :] / acc_win[dh:dh + 1, :]

    groups = [imp[8 * k:8 * k + 8, :] for k in range(ns // 8)]
    ranks = [jnp.zeros((8, TQ), jnp.int32) for _ in groups]
    sub_io = lax.broadcasted_iota(jnp.int32, (8, TQ), 0)
    for jp in range(ns):
        row = imp[jp:jp + 1, :]
        for k, grp in enumerate(groups):
            if 8 * k > jp:
                beats = (row >= grp).astype(jnp.int32)
            elif 8 * k + 7 < jp:
                beats = (row > grp).astype(jnp.int32)
            else:
                beats = jnp.where(sub_io > jp - 8 * k, (row >= grp).astype(jnp.int32),
                                  (row > grp).astype(jnp.int32))
            ranks[k] = ranks[k] + beats
    rank = jnp.concatenate(ranks, axis=0)
    sel_rows = jnp.where(rank < top_n, 0.0, -MASK_BIG).astype(BF16)

    row_used = jnp.max(jnp.where(rank < top_n, 1.0, 0.0), axis=1, keepdims=True)
    blocks_per_chunk = KC // SLC_LEN
    n_full = jnp.int32(0)
    for c in range(ns // blocks_per_chunk):
        used = jnp.max(row_used[c * blocks_per_chunk:(c + 1) * blocks_per_chunk, :]) > 0.0
        chunks_ref[n_full] = c
        n_full = n_full + jnp.where(used & (c < qi), 1, 0)
    chunks_ref[n_full] = qi
    if ns < SEL_ROWS:
        sel_rows = jnp.concatenate([sel_rows, jnp.zeros((SEL_ROWS - ns, TQ), BF16)], axis=0)
    for r in range(R):
        qsel_ref[:, lanes(r)] = jnp.concatenate(
            [qT_ref[0, r * dh:(r + 1) * dh, :], sel_rows, arows_ref[0, r], pad_sel], axis=0)

    def sel_scores(c):
        start = pl.multiple_of(c * KC, KC)
        return _dot(ksel_ref[0, 0, pl.ds(start, KC), :], qsel_ref[...])

    def sel_softmax(slot, mask):
        sc = s_buf[slot]
        if mask is not None:
            sc = jnp.where(mask, sc, NEG_INF)
        m_old = m_ref[...]
        m_new = jnp.maximum(m_old, jnp.max(sc, axis=0, keepdims=True))
        p_buf[slot] = jnp.exp2(sc - m_new).astype(BF16)
        a_buf[slot] = jnp.exp2(m_old - m_new)
        m_ref[...] = m_new

    def sel_values(c, slot):
        acc_ref[...] = a_buf[slot] * acc_ref[...] + _dot(vselT_ref[0, 0, c], p_buf[slot])

    m_ref[...] = jnp.full(m_ref.shape, NEG_INF, F32)
    acc_ref[...] = jnp.zeros(acc_ref.shape, F32)
    p_buf[1] = jnp.zeros(p_buf.shape[1:], BF16)
    a_buf[1] = jnp.ones(a_buf.shape[1:], F32)
    def chunk_at(k):
        return chunks_ref[jnp.maximum(k, 0)]

    s_buf[0] = sel_scores(chunk_at(0))

    def sel_pair(i, carry):
        k = 2 * i
        s_buf[1] = sel_scores(chunk_at(k + 1))
        sel_softmax(0, None)
        sel_values(chunk_at(k - 1), 1)
        s_buf[0] = sel_scores(chunk_at(k + 2))
        sel_softmax(1, None)
        sel_values(chunk_at(k), 0)
        return carry

    lax.fori_loop(0, n_full >> 1, sel_pair, 0)

    @pl.when((n_full & 1) == 1)
    def _():
        sel_softmax(0, None)
        s_buf[1] = sel_scores(qi)
        sel_values(chunk_at(n_full - 2), 1)
        sel_softmax(1, causal)
        sel_values(chunk_at(n_full - 1), 0)
        sel_values(qi, 1)

    @pl.when((n_full & 1) == 0)
    def _():
        sel_softmax(0, causal)
        sel_values(chunk_at(n_full - 1), 1)
        sel_values(qi, 0)

    acc = acc_ref[...]
    o_sel = acc[:dh, :] / acc[dh:dh + 1, :]
    for r in range(R):
        o = (gT_ref[0, r:r + 1, :] * ocmp_ref[:, lanes(r)]
             + gT_ref[0, R + r:R + r + 1, :] * o_sel[:, lanes(r)]
             + gT_ref[0, 2 * R + r:2 * R + r + 1, :] * owin_ref[:, lanes(r)])
        oT_ref[0, r * dh:(r + 1) * dh, :] = o.astype(BF16)


def _attention(qT, gT, kc, vcT, ksel, vselT, kwin, vwinT, arows, ovT):
    B, HD, S = qT.shape
    G, R, dh = N_KV_GROUPS, HEADS_PER_GROUP, HEAD_DIM
    nc = kc.shape[2]
    ns = S // SLC_LEN
    n_kc = S // KC
    return pl.pallas_call(
        functools.partial(_attn_kernel, ns=ns, nc=nc, top_n=min(SLC_TOP_N, ns)),
        grid=(B, G, S // TQ),
        in_specs=[
            pl.BlockSpec((1, R * dh, TQ), lambda b, g, i: (b, g, i)),
            pl.BlockSpec((1, gT.shape[1] // G, TQ), lambda b, g, i: (b, g, i)),
            pl.BlockSpec((1, 1, nc, QWIN_ROWS), lambda b, g, i: (b, g, 0, 0)),
            pl.BlockSpec((1, 1, dh, nc), lambda b, g, i: (b, g, 0, 0)),
            pl.BlockSpec((1, 1, S, QSEL_ROWS), lambda b, g, i: (b, g, 0, 0)),
            pl.BlockSpec((1, 1, n_kc, V_ROWS, KC), lambda b, g, i: (b, g, 0, 0, 0)),
            pl.BlockSpec((1, 1, S, QWIN_ROWS), lambda b, g, i: (b, g, 0, 0)),
            pl.BlockSpec((1, 1, n_kc, V_ROWS, KC), lambda b, g, i: (b, g, 0, 0, 0)),
            pl.BlockSpec((1, R, ALIBI_ROWS, TQ), lambda b, g, i: (g, 0, 0, 0)),
            pl.BlockSpec(ovT.shape, lambda b, g, i: (0, 0)),
        ],
        out_specs=pl.BlockSpec((1, R * dh, TQ), lambda b, g, i: (b, g, i)),
        out_shape=jax.ShapeDtypeStruct((B, HD, S), BF16),
        scratch_shapes=[
            pltpu.VMEM((QSEL_ROWS, R * TQ), BF16),
            pltpu.VMEM((QWIN_ROWS, R * TQ), BF16),
            pltpu.VMEM((1, R * TQ), F32),
            pltpu.VMEM((V_ROWS, R * TQ), F32),
            pltpu.VMEM((dh, R * TQ), F32),
            pltpu.VMEM((dh, R * TQ), F32),
            pltpu.VMEM((2, KC, R * TQ), F32),
            pltpu.VMEM((2, KC, R * TQ), BF16),
            pltpu.VMEM((2, 1, R * TQ), F32),
            pltpu.SMEM((n_kc + 1,), jnp.int32),
        ],
        compiler_params=_params(("parallel", "parallel", "arbitrary")),
        name="nsa_attention",
    )(qT, gT, kc, vcT, ksel, vselT, kwin, vwinT, arows, ovT)


def _oproj_kernel(x_ref, oT_ref, mod_ref, ng_ref, w_ref, o_ref):
    y = _dot_tn(oT_ref[0], w_ref[...])
    o_ref[0] = _post(x_ref[0], y, mod_ref, ng_ref, 1, 1.0)


def _oproj(x, oT, mod, ng, w_out):
    B, S, D = x.shape
    tm = min(TOKEN_TILE, S)
    return pl.pallas_call(
        _oproj_kernel,
        grid=(B, S // tm),
        in_specs=[
            pl.BlockSpec((1, tm, D), lambda b, i: (b, i, 0)),
            pl.BlockSpec((1, oT.shape[1], tm), lambda b, i: (b, 0, i)),
            pl.BlockSpec((1, 9, D), lambda b, i: (b, 0, 0)),
            pl.BlockSpec((6, D), lambda b, i: (0, 0)),
            _resident(w_out.shape),
        ],
        out_specs=pl.BlockSpec((1, tm, D), lambda b, i: (b, i, 0)),
        out_shape=jax.ShapeDtypeStruct(x.shape, F32),
        compiler_params=_params(("parallel", "parallel")),
        name="oproj",
    )(x, oT, mod, ng, w_out)


def _split3(v):
    hi = v.astype(BF16)
    r1 = v - hi.astype(F32)
    mid = r1.astype(BF16)
    lo = (r1 - mid.astype(F32)).astype(BF16)
    return hi, mid, lo


def _alibi_rows():
    h = jnp.arange(1, N_HEADS + 1, dtype=F32)
    slopes = jnp.exp2(-8.0 * h / N_HEADS).reshape(N_KV_GROUPS, HEADS_PER_GROUP)
    parts = _split3(slopes * LOG2E)
    rows = jnp.stack(list(parts) * 2 + [jnp.zeros_like(parts[0])] * (ALIBI_ROWS - 6), axis=-1)
    return jnp.broadcast_to(rows[..., None], rows.shape + (TQ,))


def _overlap_t(ns, nc, n_cmp):
    i = jnp.arange(nc)[None, :]
    j = jnp.arange(ns)[:, None]
    start = i * CMP_STRIDE
    ov = (start < (j + 1) * SLC_LEN) & (start + CMP_LEN - 1 >= j * SLC_LEN) & (i < n_cmp)
    return ov.astype(BF16)


def _pad_cols(w, width):
    return jnp.pad(w, ((0, 0), (0, width - w.shape[1])))


def kernel(x, c, ada_w, ada_b, norm_g, ffn_w_in, ffn_w_out, a_w_in, a_conv, a_w_out,
           kv_norm_g, kv_ada_w, kv_ada_b, kv_w, cmp_pos, cmp_w1, cmp_w2, b_w_in, b_w_out):
    B, S, D = x.shape
    G, R, dh = N_KV_GROUPS, HEADS_PER_GROUP, HEAD_DIM
    n_a = DEPTH // 2
    assert S % TQ == 0 and WINDOW == 2 * KC and TQ == KC
    assert D == N_HEADS * dh

    mods = _ada(c, ada_w, ada_b, tn=2304).reshape(DEPTH, B, 9, D)
    kvmod = _ada(c, kv_ada_w[None], kv_ada_b[None], tn=1024).reshape(B, 2, D)

    arows = _alibi_rows()
    nc = S // CMP_STRIDE
    n_cmp = (S - CMP_LEN) // CMP_STRIDE + 1
    ovT = _overlap_t(S // SLC_LEN, nc, n_cmp)

    kv_state = None
    for layer in range(DEPTH):
        mod = mods[layer]
        ng = norm_g[layer].reshape(6, D)
        if layer == n_a:
            gcols = G * dh
            w0 = kv_w[:, :2 * gcols].astype(BF16)
            k_sel = kv_w[:, 2 * gcols:3 * gcols]
            k_win = kv_w[:, 4 * gcols:5 * gcols]
            wks = jnp.concatenate(
                [_pad_cols(k_sel[:, g * dh:(g + 1) * dh], QSEL_ROWS) for g in range(G)], axis=1)
            wkw = jnp.concatenate(
                [_pad_cols(k_win[:, g * dh:(g + 1) * dh], QWIN_ROWS) for g in range(G)], axis=1)
            wvsT = kv_w[:, 3 * gcols:4 * gcols].T.astype(BF16)
            wvwT = kv_w[:, 5 * gcols:6 * gcols].T.astype(BF16)
            kv0, ksel, kwin, vselT, vwinT = _build_kv(
                x, kvmod, kv_norm_g, w0, wks.astype(BF16), wkw.astype(BF16), wvsT, wvwT)

            def chunked(a):
                a = a.reshape(B, S, G, dh).transpose(0, 2, 1, 3)
                return a.reshape(B, G, nc, CMP_STRIDE * dh)

            half = CMP_STRIDE * dh
            kc, vcT = _compress(
                chunked(kv0[..., :gcols]), chunked(kv0[..., gcols:]),
                cmp_pos.reshape(4, half),
                cmp_w1.reshape(2, 2, half, cmp_w1.shape[-1]).astype(BF16),
                _pad_cols(cmp_w2[0], QWIN_ROWS).astype(BF16),
                cmp_w2[1].T.astype(BF16))
            kv_state = (kc, vcT, ksel, vselT, kwin, vwinT)

        x = _ffn(x, mod, ng, ffn_w_in[layer, 0].astype(BF16), ffn_w_out[layer, 0].astype(BF16), 0)
        if layer < n_a:
            x = _conv_mixer(x, mod, ng, a_w_in[layer].astype(BF16), a_conv[layer],
                            a_w_out[layer].astype(BF16))
        else:
            w_in = b_w_in[layer - n_a]
            wqT = w_in[:, :N_HEADS * dh].T.astype(BF16)
            wg = w_in[:, N_HEADS * dh:].reshape(D, N_BRANCH, G, R)
            wg = jnp.pad(wg.transpose(2, 1, 3, 0), ((0, 0), (0, 1), (0, 0), (0, 0)))
            wgT = wg.reshape(G * (N_BRANCH + 1) * R, D).astype(BF16)
            qT, gT = _qproj(x, mod, ng, wqT, wgT)
            kc, vcT, ksel, vselT, kwin, vwinT = kv_state
            oT = _attention(qT, gT, kc, vcT, ksel, vselT, kwin, vwinT, arows, ovT)
            x = _oproj_ffn(x, oT, mod, ng, b_w_out[layer - n_a].astype(BF16),
                           ffn_w_in[layer, 1].astype(BF16), ffn_w_out[layer, 1].astype(BF16))
            continue
        x = _ffn(x, mod, ng, ffn_w_in[layer, 1].astype(BF16), ffn_w_out[layer, 1].astype(BF16), 2)
    return x
```
